```python
import jax, jax.numpy as jnp
from jax import lax
import numpy as np

D_MODEL = 1024
BATCH = 8
SEQ = 2048
DEPTH = 1

SB_HEADS = 8
SB_HEAD_DIM = 64
SB_WIDTH = SB_HEADS * SB_HEAD_DIM
MLA_HEADS = 4
MLA_NOPE_DIM = 128
MLA_ROPE_DIM = 64
MLA_QK_DIM = MLA_NOPE_DIM + MLA_ROPE_DIM
MLA_V_DIM = 128
MLA_Q_RANK = 384
MLA_KV_RANK = 256
MLA_WIDTH = MLA_HEADS * MLA_V_DIM
MIX_WIDTH = SB_WIDTH + MLA_WIDTH
IN_PROJ_WIDTH = 3 * SB_WIDTH + MLA_Q_RANK + MLA_KV_RANK + MLA_ROPE_DIM
D_FF = -(-8 * D_MODEL // (3 * 256)) * 256
N_MOD = 6
BLOCK_Q = 128
ROPE_THETA = 10000.0
EPS = 1e-6

kernel_name = "hymba_sb_mla_adaln_layer"


def rms_norm(x, g):
    xf = x.astype(jnp.float32)
    y = xf * lax.rsqrt(jnp.mean(xf * xf, axis=-1, keepdims=True) + EPS)
    return (y * g.astype(jnp.float32)).astype(x.dtype)


def apply_rope(x, positions):
    half = x.shape[-1] // 2
    freqs = 1.0 / (ROPE_THETA ** (jnp.arange(half, dtype=jnp.float32) / half))
    ang = positions.astype(jnp.float32)[:, :, None, None] * freqs
    cos, sin = jnp.cos(ang), jnp.sin(ang)
    xf = x.astype(jnp.float32)
    x1, x2 = xf[..., :half], xf[..., half:]
    return jnp.concatenate([x1 * cos - x2 * sin, x2 * cos + x1 * sin], axis=-1).astype(x.dtype)


def stick_breaking_attention(q, k, v):
    S = q.shape[1]
    scale = SB_HEAD_DIM ** -0.5
    outs = []
    for i in range(S // BLOCK_Q):
        q0 = i * BLOCK_Q
        kend = q0 + BLOCK_Q
        z = jnp.einsum('bthd,bshd->bhts', q[:, q0:kend], k[:, :kend]).astype(jnp.float32) * scale
        t_idx = q0 + jnp.arange(BLOCK_Q)[:, None]
        s_idx = jnp.arange(kend)[None, :]
        strict = s_idx < t_idx
        log_beta = jax.nn.log_sigmoid(z)
        log_1m = jnp.where(strict, jax.nn.log_sigmoid(-z), 0.0)
        after = lax.cumsum(log_1m, axis=3, reverse=True) - log_1m
        w = jnp.where(strict, jnp.exp(log_beta + after), 0.0)
        outs.append(jnp.einsum('bhts,bshd->bthd', w.astype(v.dtype), v[:, :kend]))
    return jnp.concatenate(outs, axis=1)


def causal_softmax_attention(q, k, v):
    S = q.shape[1]
    scale = MLA_QK_DIM ** -0.5
    outs = []
    for i in range(S // BLOCK_Q):
        q0 = i * BLOCK_Q
        kend = q0 + BLOCK_Q
        s = jnp.einsum('bthd,bshd->bhts', q[:, q0:kend], k[:, :kend]).astype(jnp.float32) * scale
        causal = jnp.arange(kend)[None, :] <= (q0 + jnp.arange(BLOCK_Q))[:, None]
        p = jax.nn.softmax(jnp.where(causal, s, -jnp.inf), axis=-1)
        outs.append(jnp.einsum('bhts,bshd->bthd', p.astype(v.dtype), v[:, :kend]))
    return jnp.concatenate(outs, axis=1)


def setup_inputs(seed: int = 0) -> dict:
    key = jax.random.key(seed)
    ks = jax.random.split(key, 24)
    f32 = jnp.float32

    def w(k, shape, fan_in):
        return jax.random.normal(k, shape, f32) * (fan_in ** -0.5)

    def gain(k, n):
        return 1.0 + 0.01 * jax.random.normal(k, (DEPTH, n), f32)

    x = jax.random.normal(ks[0], (BATCH, SEQ, D_MODEL), f32)
    c = jax.random.normal(ks[1], (BATCH, D_MODEL), f32)
    offsets = jax.random.randint(ks[2], (BATCH, 1), 0, 512, dtype=jnp.int32)
    positions = offsets + jnp.arange(SEQ, dtype=jnp.int32)[None, :]
    return {
        "x": x,
        "c": c,
        "positions": positions,
        "w_ada": w(ks[3], (DEPTH, D_MODEL, N_MOD * D_MODEL), D_MODEL) * 0.5,
        "b_ada": 0.01 * jax.random.normal(ks[4], (DEPTH, N_MOD * D_MODEL), f32),
        "norm_attn": gain(ks[5], D_MODEL),
        "norm_ffn": gain(ks[6], D_MODEL),
        "w_in": w(ks[7], (DEPTH, D_MODEL, IN_PROJ_WIDTH), D_MODEL),
        "q_a_norm": gain(ks[8], MLA_Q_RANK),
        "w_q_up": w(ks[9], (DEPTH, MLA_Q_RANK, MLA_HEADS * MLA_QK_DIM), MLA_Q_RANK),
        "kv_a_norm": gain(ks[10], MLA_KV_RANK),
        "w_kv_up": w(ks[11], (DEPTH, MLA_KV_RANK, MLA_HEADS * (MLA_NOPE_DIM + MLA_V_DIM)), MLA_KV_RANK),
        "q_norm": gain(ks[12], MLA_QK_DIM),
        "k_nope_norm": gain(ks[13], MLA_NOPE_DIM),
        "k_rope_norm": gain(ks[14], MLA_ROPE_DIM),
        "out_norm_sb": gain(ks[15], SB_WIDTH),
        "out_norm_mla": gain(ks[16], MLA_WIDTH),
        "w_out": w(ks[17], (DEPTH, MIX_WIDTH, D_MODEL), MIX_WIDTH),
        "w_gate": w(ks[18], (DEPTH, D_MODEL, D_FF), D_MODEL),
        "w_up": w(ks[19], (DEPTH, D_MODEL, D_FF), D_MODEL),
        "w_down": w(ks[20], (DEPTH, D_FF, D_MODEL), D_FF),
    }


def reference(x, c, positions, w_ada, b_ada, norm_attn, norm_ffn, w_in, q_a_norm, w_q_up,
              kv_a_norm, w_kv_up, q_norm, k_nope_norm, k_rope_norm, out_norm_sb, out_norm_mla,
              w_out, w_gate, w_up, w_down):
    B, S, _ = x.shape
    for l in range(DEPTH):
        mod = jax.nn.silu(c) @ w_ada[l] + b_ada[l]
        shift1, scale1, gate1, shift2, scale2, gate2 = [m[:, None, :] for m in jnp.split(mod, N_MOD, axis=-1)]

        h = rms_norm(x, norm_attn[l]) * (1.0 + scale1) + shift1
        proj = h @ w_in[l]
        cuts = np.cumsum([SB_WIDTH, SB_WIDTH, SB_WIDTH, MLA_Q_RANK, MLA_KV_RANK]).tolist()
        q_sb, k_sb, v_sb, c_q, c_kv, k_rope = jnp.split(proj, cuts, axis=-1)

        q_sb = q_sb.reshape(B, S, SB_HEADS, SB_HEAD_DIM)
        k_sb = k_sb.reshape(B, S, SB_HEADS, SB_HEAD_DIM)
        v_sb = v_sb.reshape(B, S, SB_HEADS, SB_HEAD_DIM)
        o_sb = stick_breaking_attention(q_sb, k_sb, v_sb).reshape(B, S, SB_WIDTH)

        q = (rms_norm(c_q, q_a_norm[l]) @ w_q_up[l]).reshape(B, S, MLA_HEADS, MLA_QK_DIM)
        kv = (rms_norm(c_kv, kv_a_norm[l]) @ w_kv_up[l]).reshape(B, S, MLA_HEADS, MLA_NOPE_DIM + MLA_V_DIM)
        k_nope, v_mla = kv[..., :MLA_NOPE_DIM], kv[..., MLA_NOPE_DIM:]
        q = rms_norm(q, q_norm[l])
        k_nope = rms_norm(k_nope, k_nope_norm[l])
        k_rope = rms_norm(k_rope, k_rope_norm[l])[:, :, None, :]
        q = jnp.concatenate([q[..., :MLA_NOPE_DIM], apply_rope(q[..., MLA_NOPE_DIM:], positions)], axis=-1)
        k_rope = jnp.broadcast_to(apply_rope(k_rope, positions), (B, S, MLA_HEADS, MLA_ROPE_DIM))
        k = jnp.concatenate([k_nope, k_rope], axis=-1)
        o_mla = causal_softmax_attention(q, k, v_mla).reshape(B, S, MLA_WIDTH)

        mixed = jnp.concatenate([rms_norm(o_sb, out_norm_sb[l]), rms_norm(o_mla, out_norm_mla[l])], axis=-1)
        x = x + gate1 * (mixed @ w_out[l])

        h = rms_norm(x, norm_ffn[l]) * (1.0 + scale2) + shift2
        ffn = (jax.nn.silu(h @ w_gate[l]) * (h @ w_up[l])) @ w_down[l]
        x = x + gate2 * ffn
    return x
```

```python
import functools

import numpy as np
import jax
import jax.numpy as jnp
from jax import lax
from jax.experimental import pallas as pl
from jax.experimental.pallas import tpu as pltpu

F32 = jnp.float32
BF16 = jnp.bfloat16

D_MODEL = 1024
SB_HEADS = 8
SB_HEAD_DIM = 64
SB_WIDTH = SB_HEADS * SB_HEAD_DIM
MLA_HEADS = 4
MLA_NOPE_DIM = 128
MLA_ROPE_DIM = 64
MLA_QK_DIM = MLA_NOPE_DIM + MLA_ROPE_DIM
MLA_V_DIM = 128
MLA_Q_RANK = 384
MLA_KV_RANK = 256
MLA_WIDTH = MLA_HEADS * MLA_V_DIM
N_MOD = 6
ROPE_THETA = 10000.0
EPS = 1e-6

LANES = 128
VMEM_LIMIT_BYTES = 56 * 1024 * 1024

ADALN_TN = 1024
PROJ_TS = 512
ATT_TQ = 256
ATT_TK = 256
FFN_TM = 512
FFN_TN = 256

_C_QSB = 0
_C_KSB = SB_WIDTH
_C_VSB = 2 * SB_WIDTH
_C_CQ = 3 * SB_WIDTH
_C_CKV = _C_CQ + MLA_Q_RANK
_C_KR = _C_CKV + MLA_KV_RANK
_C_KRS = _C_KR + LANES
_N_PROJ = _C_KRS + LANES
_Q_HEAD_W = 2 * LANES


def _nt_dot(a, b):
    return lax.dot_general(a, b, (((1,), (1,)), ((), ())), preferred_element_type=F32)


def _dot(a, b):
    return jnp.dot(a, b, preferred_element_type=F32)


def _mean_sq(x, n):
    return jnp.sum(x * x, axis=-1, keepdims=True) * (1.0 / n)


def _adaln_kernel(c_ref, w_ref, b_ref, o_ref):
    c = c_ref[...]
    sc = c / (1.0 + jnp.exp(-c))
    w = w_ref[...]
    s_hi = sc.astype(BF16)
    s_lo = (sc - s_hi.astype(F32)).astype(BF16)
    w_hi = w.astype(BF16)
    w_lo = (w - w_hi.astype(F32)).astype(BF16)
    acc = _dot(s_hi, w_hi) + _dot(s_lo, w_hi) + _dot(s_hi, w_lo)
    o_ref[...] = acc + b_ref[...]


def _adaln(c, w_ada, b_ada):
    b, d = c.shape
    n = w_ada.shape[1]
    return pl.pallas_call(
        _adaln_kernel,
        grid=(n // ADALN_TN,),
        in_specs=[
            pl.BlockSpec((b, d), lambda j: (0, 0)),
            pl.BlockSpec((d, ADALN_TN), lambda j: (0, j)),
            pl.BlockSpec((1, ADALN_TN), lambda j: (0, j)),
        ],
        out_specs=pl.BlockSpec((b, ADALN_TN), lambda j: (0, j)),
        out_shape=jax.ShapeDtypeStruct((b, n), F32),
        compiler_params=pltpu.CompilerParams(
            dimension_semantics=("arbitrary",), vmem_limit_bytes=VMEM_LIMIT_BYTES),
        name="adaln",
    )(c, w_ada, b_ada.reshape(1, n))


def _inproj_kernel(x_ref, pos_ref, mod_ref, gattn_ref, w1_ref, gqa_ref, wq_ref, gkva_ref, wkv_ref,
                   gqn_ref, gqr_ref, gkn_ref, gk1_ref, gk2_ref, freq_ref, sign_ref,
                   qkv_ref, qm_ref, kn_ref, kr_ref, vm_ref):
    x = x_ref[0]
    shift1 = mod_ref[0, 0:1, :]
    scale1 = mod_ref[0, 1:2, :]
    r = lax.rsqrt(_mean_sq(x, D_MODEL) + EPS)
    h = (x * r) * (gattn_ref[...] * (1.0 + scale1)) + shift1
    proj = _dot(h.astype(BF16), w1_ref[...])

    qkv_ref[0, :, 0:SB_WIDTH] = (proj[:, _C_QSB:_C_KSB] * (SB_HEAD_DIM ** -0.5)).astype(BF16)
    qkv_ref[0, :, SB_WIDTH:3 * SB_WIDTH] = proj[:, _C_KSB:_C_CQ].astype(BF16)

    pos = pos_ref[0].astype(F32)
    ang = pos * freq_ref[...]
    cos_t = jnp.cos(ang)
    sin_t = jnp.sin(ang) * sign_ref[...]

    cq = proj[:, _C_CQ:_C_CKV]
    cqn = cq * lax.rsqrt(_mean_sq(cq, MLA_Q_RANK) + EPS) * gqa_ref[...]
    y = _dot(cqn.astype(BF16), wq_ref[...])
    lane = lax.broadcasted_iota(jnp.int32, (1, LANES), 1)
    rope_q = jnp.where(lane < MLA_ROPE_DIM, cos_t, sin_t) * gqr_ref[...]
    for hd in range(MLA_HEADS):
        yn = y[:, hd * _Q_HEAD_W: hd * _Q_HEAD_W + LANES]
        yr = y[:, hd * _Q_HEAD_W + LANES: (hd + 1) * _Q_HEAD_W]
        ssq = jnp.sum(yn * yn, axis=-1, keepdims=True) + 0.5 * jnp.sum(yr * yr, axis=-1, keepdims=True)
        rq = lax.rsqrt(ssq * (1.0 / MLA_QK_DIM) + EPS)
        qm_ref[0, :, hd * _Q_HEAD_W: hd * _Q_HEAD_W + LANES] = (yn * rq * gqn_ref[...]).astype(BF16)
        qm_ref[0, :, hd * _Q_HEAD_W + LANES: (hd + 1) * _Q_HEAD_W] = (yr * rq * rope_q).astype(BF16)

    ckv = proj[:, _C_CKV:_C_KR]
    ckvn = ckv * lax.rsqrt(_mean_sq(ckv, MLA_KV_RANK) + EPS) * gkva_ref[...]
    kv = _dot(ckvn.astype(BF16), wkv_ref[...])
    for hd in range(MLA_HEADS):
        kn = kv[:, hd * LANES:(hd + 1) * LANES]
        kn_ref[0, :, hd * LANES:(hd + 1) * LANES] = (
            kn * lax.rsqrt(_mean_sq(kn, MLA_NOPE_DIM) + EPS) * gkn_ref[...]).astype(BF16)
    vm_ref[0] = kv[:, MLA_WIDTH:].astype(BF16)

    z1 = proj[:, _C_KR:_C_KRS]
    z2 = proj[:, _C_KRS:_N_PROJ]
    rk = lax.rsqrt(_mean_sq(z1, LANES) + EPS)
    kr_ref[0] = (rk * (z1 * (gk1_ref[...] * cos_t) + z2 * (gk2_ref[...] * sin_t))).astype(BF16)


def _inproj(x, pos3, mod3, gattn, w1, gqa, wq, gkva, wkv, gqn, gqr, gkn, gk1, gk2, freq, sign):
    b, s, d = x.shape
    ts = PROJ_TS
    const = lambda shape: pl.BlockSpec(shape, lambda i, j: (0,) * len(shape), pipeline_mode=pl.Buffered(1))
    tok = lambda w: pl.BlockSpec((1, ts, w), lambda i, j: (i, j, 0))
    return pl.pallas_call(
        _inproj_kernel,
        grid=(b, s // ts),
        in_specs=[
            tok(d),
            tok(1),
            pl.BlockSpec((1, N_MOD, d), lambda i, j: (i, 0, 0)),
            const((1, d)),
            const(w1.shape),
            const((1, MLA_Q_RANK)),
            const(wq.shape),
            const((1, MLA_KV_RANK)),
            const(wkv.shape),
            const((1, LANES)), const((1, LANES)), const((1, LANES)), const((1, LANES)), const((1, LANES)),
            const((1, LANES)), const((1, LANES)),
        ],
        out_specs=[tok(3 * SB_WIDTH), tok(MLA_HEADS * _Q_HEAD_W), tok(MLA_HEADS * LANES), tok(LANES),
                   tok(MLA_WIDTH)],
        out_shape=[
            jax.ShapeDtypeStruct((b, s, 3 * SB_WIDTH), BF16),
            jax.ShapeDtypeStruct((b, s, MLA_HEADS * _Q_HEAD_W), BF16),
            jax.ShapeDtypeStruct((b, s, MLA_HEADS * LANES), BF16),
            jax.ShapeDtypeStruct((b, s, LANES), BF16),
            jax.ShapeDtypeStruct((b, s, MLA_WIDTH), BF16),
        ],
        compiler_params=pltpu.CompilerParams(
            dimension_semantics=("arbitrary", "arbitrary"), vmem_limit_bytes=VMEM_LIMIT_BYTES),
        name="inproj",
    )(x, pos3, mod3, gattn, w1, gqa, wq, gkva, wkv, gqn, gqr, gkn, gk1, gk2, freq, sign)


def _sb_kernel(q_ref, k_ref, v_ref, tri_ref, o_ref, acc_ref, run_ref):
    i = pl.program_id(2)
    tq, tk = ATT_TQ, ATT_TK
    q = q_ref[0]
    lane = lax.broadcasted_iota(jnp.int32, (1, LANES), 1)
    first = lane < SB_HEAD_DIM
    row = lax.broadcasted_iota(jnp.int32, (tq, tk), 0)
    col = lax.broadcasted_iota(jnp.int32, (tq, tk), 1)
    strict = col < row

    def block(hh, qh, j, masked):
        k = k_ref[0, pl.ds(j * tk, tk), :]
        v = v_ref[0, pl.ds(j * tk, tk), :]
        z = _nt_dot(qh, k)
        sp = jnp.maximum(z, 0.0) + jnp.log(1.0 + jnp.exp(-jnp.abs(z)))
        if masked:
            sp = jnp.where(strict, sp, 0.0)
        hi = sp.astype(BF16)
        lo = (sp - hi.astype(F32)).astype(BF16)
        cin = _dot(jnp.concatenate([hi, lo], axis=1), tri_ref[...])
        run = run_ref[hh]
        w = jnp.exp(z - cin - jnp.concatenate([run] * (tk // LANES), axis=1))
        if masked:
            w = jnp.where(strict, w, 0.0)
        acc_ref[hh] += _dot(w.astype(BF16), v)
        run_ref[hh] = run + jnp.broadcast_to(cin[:, 0:1], (tq, LANES))

    for hh in range(2):
        qh = jnp.where(first if hh == 0 else jnp.logical_not(first), q, jnp.zeros_like(q))
        acc_ref[hh] = jnp.zeros((tq, LANES), F32)
        run_ref[hh] = jnp.zeros((tq, LANES), F32)
        block(hh, qh, i, True)

        def body(t, carry, hh=hh, qh=qh):
            block(hh, qh, i - 1 - t, False)
            return carry

        lax.fori_loop(0, i, body, 0)

    o_ref[0] = jnp.where(first, acc_ref[0], acc_ref[1]).astype(BF16)


def _sb_attention(qkv, tri):
    b, s, _ = qkv.shape
    n_pairs = SB_WIDTH // LANES
    return pl.pallas_call(
        _sb_kernel,
        grid=(b, n_pairs, s // ATT_TQ),
        in_specs=[
            pl.BlockSpec((1, ATT_TQ, LANES), lambda bi, p, i: (bi, i, p)),
            pl.BlockSpec((1, s, LANES), lambda bi, p, i: (bi, 0, n_pairs + p)),
            pl.BlockSpec((1, s, LANES), lambda bi, p, i: (bi, 0, 2 * n_pairs + p)),
            pl.BlockSpec(tri.shape, lambda bi, p, i: (0, 0), pipeline_mode=pl.Buffered(1)),
        ],
        out_specs=pl.BlockSpec((1, ATT_TQ, LANES), lambda bi, p, i: (bi, i, p)),
        out_shape=jax.ShapeDtypeStruct((b, s, SB_WIDTH), BF16),
        scratch_shapes=[pltpu.VMEM((2, ATT_TQ, LANES), F32), pltpu.VMEM((2, ATT_TQ, LANES), F32)],
        compiler_params=pltpu.CompilerParams(
            dimension_semantics=("arbitrary", "arbitrary", "arbitrary"), vmem_limit_bytes=VMEM_LIMIT_BYTES),
        name="sb_attn",
    )(qkv, qkv, qkv, tri)


_NEG_BIG = -1e30


def _mla_kernel(q_ref, kn_ref, kr_ref, v_ref, o_ref, acc_ref, m_ref, l_ref):
    i = pl.program_id(2)
    tq, tk = ATT_TQ, ATT_TK
    c = (MLA_QK_DIM ** -0.5) * float(np.log2(np.e))
    q = q_ref[0]
    qn = q[:, :LANES]
    qr = q[:, LANES:]
    row = lax.broadcasted_iota(jnp.int32, (tq, tk), 0)
    col = lax.broadcasted_iota(jnp.int32, (tq, tk), 1)
    causal = col <= row

    def block(j, masked):
        kn = kn_ref[0, pl.ds(j * tk, tk), :]
        kr = kr_ref[0, pl.ds(j * tk, tk), :]
        v = v_ref[0, pl.ds(j * tk, tk), :]
        s = _nt_dot(qn, kn) + _nt_dot(qr, kr)
        if masked:
            s = jnp.where(causal, s, _NEG_BIG)
        m_old = m_ref[...]
        m_new = jnp.maximum(m_old, jnp.max(s, axis=-1, keepdims=True))
        alpha = jnp.exp2((m_old - m_new) * c)
        p = jnp.exp2((s - m_new) * c)
        l_ref[...] = alpha * l_ref[...] + jnp.sum(p, axis=-1, keepdims=True)
        acc_ref[...] = alpha * acc_ref[...] + _dot(p.astype(BF16), v)
        m_ref[...] = m_new

    m_ref[...] = jnp.full((tq, 1), _NEG_BIG, F32)
    l_ref[...] = jnp.zeros((tq, 1), F32)
    acc_ref[...] = jnp.zeros((tq, LANES), F32)
    block(i, True)

    def body(t, carry):
        block(i - 1 - t, False)
        return carry

    lax.fori_loop(0, i, body, 0)
    o_ref[0] = (acc_ref[...] / l_ref[...]).astype(BF16)


def _mla_attention(qm, kn, kr, vm):
    b, s, _ = qm.shape
    return pl.pallas_call(
        _mla_kernel,
        grid=(b, MLA_HEADS, s // ATT_TQ),
        in_specs=[
            pl.BlockSpec((1, ATT_TQ, _Q_HEAD_W), lambda bi, h, i: (bi, i, h)),
            pl.BlockSpec((1, s, LANES), lambda bi, h, i: (bi, 0, h)),
            pl.BlockSpec((1, s, LANES), lambda bi, h, i: (bi, 0, 0)),
            pl.BlockSpec((1, s, LANES), lambda bi, h, i: (bi, 0, h)),
        ],
        out_specs=pl.BlockSpec((1, ATT_TQ, LANES), lambda bi, h, i: (bi, i, h)),
        out_shape=jax.ShapeDtypeStruct((b, s, MLA_WIDTH), BF16),
        scratch_shapes=[pltpu.VMEM((ATT_TQ, LANES), F32), pltpu.VMEM((ATT_TQ, 1), F32),
                        pltpu.VMEM((ATT_TQ, 1), F32)],
        compiler_params=pltpu.CompilerParams(
            dimension_semantics=("arbitrary", "arbitrary", "arbitrary"), vmem_limit_bytes=VMEM_LIMIT_BYTES),
        name="mla_attn",
    )(qm, kn, kr, vm)


def _outffn_kernel(x_ref, osb_ref, omla_ref, mod_ref, gsb_ref, gmla_ref, gffn_ref,
                   wo_ref, wg_ref, wu_ref, wd_ref, out_ref, act_ref):
    gate1 = mod_ref[0, 2:3, :]
    shift2 = mod_ref[0, 3:4, :]
    scale2 = mod_ref[0, 4:5, :]
    gate2 = mod_ref[0, 5:6, :]

    osb = osb_ref[0].astype(F32)
    omla = omla_ref[0].astype(F32)
    nsb = osb * lax.rsqrt(_mean_sq(osb, SB_WIDTH) + EPS) * gsb_ref[...]
    nmla = omla * lax.rsqrt(_mean_sq(omla, MLA_WIDTH) + EPS) * gmla_ref[...]
    mixed = jnp.concatenate([nsb.astype(BF16), nmla.astype(BF16)], axis=1)
    x1 = x_ref[0] + gate1 * _dot(mixed, wo_ref[...])

    h2 = (x1 * lax.rsqrt(_mean_sq(x1, D_MODEL) + EPS)) * (gffn_ref[...] * (1.0 + scale2)) + shift2
    h2 = h2.astype(BF16)
    d_ff = wg_ref.shape[1]
    for cidx in range(d_ff // FFN_TN):
        sl = slice(cidx * FFN_TN, (cidx + 1) * FFN_TN)
        g = _dot(h2, wg_ref[:, sl])
        u = _dot(h2, wu_ref[:, sl])
        act_ref[:, sl] = (g / (1.0 + jnp.exp(-g)) * u).astype(BF16)
    out_ref[0] = x1 + gate2 * _dot(act_ref[...], wd_ref[...])


def _outffn(x, osb, omla, mod3, gsb, gmla, gffn, wo, wg, wu, wd):
    b, s, d = x.shape
    tm = FFN_TM
    d_ff = wg.shape[1]
    const = lambda shape: pl.BlockSpec(shape, lambda i, j: (0,) * len(shape), pipeline_mode=pl.Buffered(1))
    tok = lambda w: pl.BlockSpec((1, tm, w), lambda i, j: (i, j, 0))
    return pl.pallas_call(
        _outffn_kernel,
        grid=(b, s // tm),
        in_specs=[
            tok(d), tok(SB_WIDTH), tok(MLA_WIDTH),
            pl.BlockSpec((1, N_MOD, d), lambda i, j: (i, 0, 0)),
            const((1, SB_WIDTH)), const((1, MLA_WIDTH)), const((1, d)),
            const(wo.shape), const(wg.shape), const(wu.shape), const(wd.shape),
        ],
        out_specs=tok(d),
        out_shape=jax.ShapeDtypeStruct((b, s, d), F32),
        scratch_shapes=[pltpu.VMEM((tm, d_ff), BF16)],
        compiler_params=pltpu.CompilerParams(
            dimension_semantics=("arbitrary", "arbitrary"), vmem_limit_bytes=VMEM_LIMIT_BYTES),
        name="outffn",
    )(x, osb, omla, mod3, gsb, gmla, gffn, wo, wg, wu, wd)


def _swap_halves(n):
    half = n // 2
    return np.concatenate([np.arange(half, n), np.arange(0, half)])


def _layer_params(l, w_in, w_q_up, w_kv_up, q_norm, k_rope_norm):
    swap = _swap_halves(MLA_ROPE_DIM)
    kr0 = _C_CKV + MLA_KV_RANK
    kr = kr0 + np.arange(MLA_ROPE_DIM)
    krs = kr0 + swap
    w1_idx = np.concatenate([np.arange(kr0), kr, kr, krs, krs])
    w1 = w_in[l][:, w1_idx].astype(BF16)

    q_idx = []
    for hd in range(MLA_HEADS):
        base = hd * MLA_QK_DIM
        q_idx += [base + np.arange(MLA_NOPE_DIM), base + MLA_NOPE_DIM + np.arange(MLA_ROPE_DIM),
                  base + MLA_NOPE_DIM + swap]
    wq = w_q_up[l][:, np.concatenate(q_idx)].astype(BF16)

    per_head = MLA_NOPE_DIM + MLA_V_DIM
    kv_idx = ([hd * per_head + np.arange(MLA_NOPE_DIM) for hd in range(MLA_HEADS)]
              + [hd * per_head + MLA_NOPE_DIM + np.arange(MLA_V_DIM) for hd in range(MLA_HEADS)])
    wkv = w_kv_up[l][:, np.concatenate(kv_idx)].astype(BF16)

    gqn = q_norm[l][:MLA_NOPE_DIM].reshape(1, LANES)
    gr = q_norm[l][MLA_NOPE_DIM:]
    gqr = jnp.concatenate([gr, gr[swap]]).reshape(1, LANES)
    gk = k_rope_norm[l]
    gk1 = jnp.concatenate([gk, gk]).reshape(1, LANES)
    gk2 = jnp.concatenate([gk[swap], gk[swap]]).reshape(1, LANES)
    return w1, wq, wkv, gqn, gqr, gk1, gk2


def _constants():
    half = MLA_ROPE_DIM // 2
    freqs = 1.0 / (ROPE_THETA ** (jnp.arange(half, dtype=F32) / half))
    freq = jnp.tile(freqs, LANES // half).reshape(1, LANES)
    sign = np.tile(np.concatenate([-np.ones(half), np.ones(half)]), LANES // (2 * half))
    sign = jnp.asarray(sign, F32).reshape(1, LANES)
    tk = ATT_TK
    tri = (np.arange(tk)[:, None] >= np.arange(tk)[None, :]).astype(np.float32)
    tri = jnp.asarray(np.concatenate([tri, tri], axis=0), BF16)
    return freq, sign, tri


def kernel(x, c, positions, w_ada, b_ada, norm_attn, norm_ffn, w_in, q_a_norm, w_q_up, kv_a_norm, w_kv_up,
           q_norm, k_nope_norm, k_rope_norm, out_norm_sb, out_norm_mla, w_out, w_gate, w_up, w_down):
    b, s, d = x.shape
    depth = w_ada.shape[0]
    freq, sign, tri = _constants()
    pos3 = positions.reshape(b, s, 1)
    for l in range(depth):
        mod3 = _adaln(c, w_ada[l], b_ada[l]).reshape(b, N_MOD, d)
        w1, wq, wkv, gqn, gqr, gk1, gk2 = _layer_params(l, w_in, w_q_up, w_kv_up, q_norm, k_rope_norm)
        qkv, qm, kn, kr, vm = _inproj(
            x, pos3, mod3, norm_attn[l].reshape(1, d), w1, q_a_norm[l].reshape(1, -1), wq,
            kv_a_norm[l].reshape(1, -1), wkv, gqn, gqr, k_nope_norm[l].reshape(1, LANES), gk1, gk2, freq, sign)
        osb = _sb_attention(qkv, tri)
        omla = _mla_attention(qm, kn, kr, vm)
        x = _outffn(x, osb, omla, mod3, out_norm_sb[l].reshape(1, -1), out_norm_mla[l].reshape(1, -1),
                    norm_ffn[l].reshape(1, d), w_out[l].astype(BF16), w_gate[l].astype(BF16),
                    w_up[l].astype(BF16), w_down[l].astype(BF16))
    return x
```

```python
import numpy as np
import jax
import jax.numpy as jnp
from jax import lax
from jax.experimental import pallas as pl
from jax.experimental.pallas import tpu as pltpu

F32 = jnp.float32
BF16 = jnp.bfloat16

D_MODEL = 1024
SB_HEADS = 8
SB_HEAD_DIM = 64
SB_WIDTH = SB_HEADS * SB_HEAD_DIM
MLA_HEADS = 4
MLA_NOPE_DIM = 128
MLA_ROPE_DIM = 64
MLA_QK_DIM = MLA_NOPE_DIM + MLA_ROPE_DIM
MLA_V_DIM = 128
MLA_Q_RANK = 384
MLA_KV_RANK = 256
MLA_WIDTH = MLA_HEADS * MLA_V_DIM
N_MOD = 6
ROPE_THETA = 10000.0
EPS = 1e-6
LOG2E = float(np.log2(np.e))

LANES = 128
VMEM_LIMIT_BYTES = 56 * 1024 * 1024

ADALN_TN = 1024
PROJ_TS = 512
ATT_T = 256
ATT_UNROLL = 3
FFN_TM = 512
FFN_TN = 256
MASK_BIAS = -1e9

_C_QSB = 0
_C_KSB = SB_WIDTH
_C_VSB = 2 * SB_WIDTH
_C_CQ = 3 * SB_WIDTH
_C_CKV = _C_CQ + MLA_Q_RANK
_C_KR = _C_CKV + MLA_KV_RANK
_C_KRS = _C_KR + LANES
_N_PROJ = _C_KRS + LANES
_Q_HEAD_W = 2 * LANES


def _nt_dot(a, b):
    return lax.dot_general(a, b, (((1,), (1,)), ((), ())), preferred_element_type=F32)


def _dot(a, b):
    return jnp.dot(a, b, preferred_element_type=F32)


def _mean_sq(x, n):
    return jnp.sum(x * x, axis=-1, keepdims=True) * (1.0 / n)


def _adaln_kernel(c_ref, w_ref, b_ref, o_ref):
    c = c_ref[...]
    sc = c / (1.0 + jnp.exp(-c))
    w = w_ref[...]
    s_hi = sc.astype(BF16)
    s_lo = (sc - s_hi.astype(F32)).astype(BF16)
    w_hi = w.astype(BF16)
    w_lo = (w - w_hi.astype(F32)).astype(BF16)
    acc = _dot(s_hi, w_hi) + _dot(s_lo, w_hi) + _dot(s_hi, w_lo)
    o_ref[...] = acc + b_ref[...]


def _adaln(c, w_ada, b_ada):
    b, d = c.shape
    n = w_ada.shape[1]
    return pl.pallas_call(
        _adaln_kernel,
        grid=(n // ADALN_TN,),
        in_specs=[
            pl.BlockSpec((b, d), lambda j: (0, 0)),
            pl.BlockSpec((d, ADALN_TN), lambda j: (0, j)),
            pl.BlockSpec((1, ADALN_TN), lambda j: (0, j)),
        ],
        out_specs=pl.BlockSpec((b, ADALN_TN), lambda j: (0, j)),
        out_shape=jax.ShapeDtypeStruct((b, n), F32),
        compiler_params=pltpu.CompilerParams(
            dimension_semantics=("arbitrary",), vmem_limit_bytes=VMEM_LIMIT_BYTES),
        name="adaln",
    )(c, w_ada, b_ada.reshape(1, n))


def _inproj_kernel(x_ref, pos_ref, mod_ref, gattn_ref, w1_ref, gqa_ref, wq_ref, gkva_ref, wkv_ref,
                   gqn_ref, gqr_ref, gkn_ref, gk1_ref, gk2_ref, freq_ref, sign_ref,
                   qkv_ref, qm_ref, km_ref, vm_ref):
    x = x_ref[0]
    shift1 = mod_ref[0, 0:1, :]
    scale1 = mod_ref[0, 1:2, :]
    r = lax.rsqrt(_mean_sq(x, D_MODEL) + EPS)
    h = (x * r) * (gattn_ref[...] * (1.0 + scale1)) + shift1
    proj = _dot(h.astype(BF16), w1_ref[...])

    qkv_ref[0, :, 0:SB_WIDTH] = (proj[:, _C_QSB:_C_KSB] * (SB_HEAD_DIM ** -0.5 * LOG2E)).astype(BF16)
    qkv_ref[0, :, SB_WIDTH:3 * SB_WIDTH] = proj[:, _C_KSB:_C_CQ].astype(BF16)

    pos = pos_ref[0].astype(F32)
    ang = pos * freq_ref[...]
    cos_t = jnp.cos(ang)
    sin_t = jnp.sin(ang) * sign_ref[...]

    cq = proj[:, _C_CQ:_C_CKV]
    cqn = cq * lax.rsqrt(_mean_sq(cq, MLA_Q_RANK) + EPS) * gqa_ref[...]
    y = _dot(cqn.astype(BF16), wq_ref[...])
    lane = lax.broadcasted_iota(jnp.int32, (1, LANES), 1)
    rope_q = jnp.where(lane < MLA_ROPE_DIM, cos_t, sin_t) * gqr_ref[...]
    for hd in range(MLA_HEADS):
        yn = y[:, hd * _Q_HEAD_W: hd * _Q_HEAD_W + LANES]
        yr = y[:, hd * _Q_HEAD_W + LANES: (hd + 1) * _Q_HEAD_W]
        ssq = jnp.sum(yn * yn, axis=-1, keepdims=True) + 0.5 * jnp.sum(yr * yr, axis=-1, keepdims=True)
        rq = lax.rsqrt(ssq * (1.0 / MLA_QK_DIM) + EPS) * (MLA_QK_DIM ** -0.5 * LOG2E)
        qm_ref[0, :, hd * _Q_HEAD_W: hd * _Q_HEAD_W + LANES] = (yn * rq * gqn_ref[...]).astype(BF16)
        qm_ref[0, :, hd * _Q_HEAD_W + LANES: (hd + 1) * _Q_HEAD_W] = (yr * rq * rope_q).astype(BF16)

    z1 = proj[:, _C_KR:_C_KRS]
    z2 = proj[:, _C_KRS:_N_PROJ]
    rk = lax.rsqrt(_mean_sq(z1, LANES) + EPS)
    k_rope = (rk * (z1 * (gk1_ref[...] * cos_t) + z2 * (gk2_ref[...] * sin_t))).astype(BF16)

    ckv = proj[:, _C_CKV:_C_KR]
    ckvn = ckv * lax.rsqrt(_mean_sq(ckv, MLA_KV_RANK) + EPS) * gkva_ref[...]
    kv = _dot(ckvn.astype(BF16), wkv_ref[...])
    for hd in range(MLA_HEADS):
        kn = kv[:, hd * LANES:(hd + 1) * LANES]
        km_ref[0, :, hd * _Q_HEAD_W: hd * _Q_HEAD_W + LANES] = (
            kn * lax.rsqrt(_mean_sq(kn, MLA_NOPE_DIM) + EPS) * gkn_ref[...]).astype(BF16)
        km_ref[0, :, hd * _Q_HEAD_W + LANES: (hd + 1) * _Q_HEAD_W] = k_rope
    vm_ref[0] = kv[:, MLA_WIDTH:].astype(BF16)


def _inproj(x, pos3, mod3, gattn, w1, gqa, wq, gkva, wkv, gqn, gqr, gkn, gk1, gk2, freq, sign):
    b, s, d = x.shape
    ts = PROJ_TS
    const = lambda shape: pl.BlockSpec(shape, lambda i, j: (0,) * len(shape), pipeline_mode=pl.Buffered(1))
    tok = lambda w: pl.BlockSpec((1, ts, w), lambda i, j: (i, j, 0))
    return pl.pallas_call(
        _inproj_kernel,
        grid=(b, s // ts),
        in_specs=[
            tok(d),
            tok(1),
            pl.BlockSpec((1, N_MOD, d), lambda i, j: (i, 0, 0)),
            const((1, d)),
            const(w1.shape),
            const((1, MLA_Q_RANK)),
            const(wq.shape),
            const((1, MLA_KV_RANK)),
            const(wkv.shape),
            const((1, LANES)), const((1, LANES)), const((1, LANES)), const((1, LANES)), const((1, LANES)),
            const((1, LANES)), const((1, LANES)),
        ],
        out_specs=[tok(3 * SB_WIDTH), tok(MLA_HEADS * _Q_HEAD_W), tok(MLA_HEADS * _Q_HEAD_W), tok(MLA_WIDTH)],
        out_shape=[
            jax.ShapeDtypeStruct((b, s, 3 * SB_WIDTH), BF16),
            jax.ShapeDtypeStruct((b, s, MLA_HEADS * _Q_HEAD_W), BF16),
            jax.ShapeDtypeStruct((b, s, MLA_HEADS * _Q_HEAD_W), BF16),
            jax.ShapeDtypeStruct((b, s, MLA_WIDTH), BF16),
        ],
        compiler_params=pltpu.CompilerParams(
            dimension_semantics=("arbitrary", "arbitrary"), vmem_limit_bytes=VMEM_LIMIT_BYTES),
        name="inproj",
    )(x, pos3, mod3, gattn, w1, gqa, wq, gkva, wkv, gqn, gqr, gkn, gk1, gk2, freq, sign)


def _groups(c, nq):
    return ((0, c, 0), (1, nq - 1 - c, c + 1))


def _static_loop(n, body):
    if n > 0:
        def step(j, carry):
            body(j)
            return carry
        lax.fori_loop(0, n, step, 0, unroll=min(n, ATT_UNROLL))


def _rows(j, t):
    if isinstance(j, int):
        return pl.ds(j * t, t)
    return pl.ds(pl.multiple_of(j * t, t), t)


def _specialize(step_id, n_steps, body):
    for c in range(n_steps):
        pl.when(step_id == c)(lambda c=c: body(c))


def _sb_kernel(qa_ref, qb_ref, k_ref, v_ref, tri_ref, bias_ref, oa_ref, ob_ref,
               qs_ref, z_ref, hl_ref, w_ref, f_ref, acc_ref):
    t = ATT_T
    nq = k_ref.shape[1] // t
    lane = lax.broadcasted_iota(jnp.int32, (1, LANES), 1)
    first = lane < SB_HEAD_DIM

    for sel, q_ref in enumerate((qa_ref, qb_ref)):
        q = q_ref[0]
        qs_ref[2 * sel] = jnp.where(first, q, jnp.zeros_like(q))
        qs_ref[2 * sel + 1] = jnp.where(first, jnp.zeros_like(q), q)
    acc_ref[...] = jnp.zeros(acc_ref.shape, F32)

    def scores(sel, j, slot, diag):
        k = k_ref[0, _rows(j, t), :]
        for hh in range(2):
            z = _nt_dot(qs_ref[2 * sel + hh], k)
            if diag:
                z = z + bias_ref[...]
            sp = jnp.maximum(z, 0.0) + jnp.log(1.0 + jnp.exp2(-jnp.abs(z))) * LOG2E
            hi = sp.astype(BF16)
            z_ref[slot, hh] = z
            hl_ref[slot, hh, :, 0:t] = hi
            hl_ref[slot, hh, :, t:2 * t] = (sp - hi.astype(F32)).astype(BF16)

    def weights(slot):
        for hh in range(2):
            cin = _dot(hl_ref[slot, hh], tri_ref[...])
            w_ref[slot, hh] = jnp.exp2(z_ref[slot, hh] - cin).astype(BF16)
            f_ref[slot, hh] = jnp.exp2(-jnp.broadcast_to(cin[:, 0:1], (t, LANES)))

    def accumulate(sel, j, slot):
        v = v_ref[0, _rows(j, t), :]
        for hh in range(2):
            acc_ref[sel, hh] = acc_ref[sel, hh] * f_ref[slot, hh] + _dot(w_ref[slot, hh], v)

    def step(c):
        for sel, n_off, base in _groups(c, nq):
            _static_loop(n_off, lambda j, sel=sel, base=base: scores(sel, j, base + j, False))
            scores(sel, n_off, base + n_off, True)
        _static_loop(nq + 1, weights)
        for sel, n_off, base in _groups(c, nq):
            _static_loop(n_off + 1, lambda j, sel=sel, base=base: accumulate(sel, j, base + j))

    _specialize(pl.program_id(2), nq // 2, step)

    oa_ref[0] = jnp.where(first, acc_ref[0, 0], acc_ref[0, 1]).astype(BF16)
    ob_ref[0] = jnp.where(first, acc_ref[1, 0], acc_ref[1, 1]).astype(BF16)


def _sb_attention(qkv, tri, bias):
    b, s, _ = qkv.shape
    t = ATT_T
    nq = s // t
    n_pairs = SB_WIDTH // LANES
    const = lambda shape: pl.BlockSpec(shape, lambda bi, p, i: (0,) * len(shape), pipeline_mode=pl.Buffered(1))
    half = jax.ShapeDtypeStruct((b, s // 2, SB_WIDTH), BF16)
    return pl.pallas_call(
        _sb_kernel,
        grid=(b, n_pairs, nq // 2),
        in_specs=[
            pl.BlockSpec((1, t, LANES), lambda bi, p, i: (bi, i, p)),
            pl.BlockSpec((1, t, LANES), lambda bi, p, i: (bi, nq - 1 - i, p)),
            pl.BlockSpec((1, s, LANES), lambda bi, p, i: (bi, 0, n_pairs + p)),
            pl.BlockSpec((1, s, LANES), lambda bi, p, i: (bi, 0, 2 * n_pairs + p)),
            const(tri.shape),
            const(bias.shape),
        ],
        out_specs=[
            pl.BlockSpec((1, t, LANES), lambda bi, p, i: (bi, i, p)),
            pl.BlockSpec((1, t, LANES), lambda bi, p, i: (bi, nq // 2 - 1 - i, p)),
        ],
        out_shape=[half, half],
        scratch_shapes=[
            pltpu.VMEM((4, t, LANES), BF16),
            pltpu.VMEM((nq + 1, 2, t, t), F32),
            pltpu.VMEM((nq + 1, 2, t, 2 * t), BF16),
            pltpu.VMEM((nq + 1, 2, t, t), BF16),
            pltpu.VMEM((nq + 1, 2, t, LANES), F32),
            pltpu.VMEM((2, 2, t, LANES), F32),
        ],
        compiler_params=pltpu.CompilerParams(
            dimension_semantics=("arbitrary", "arbitrary", "arbitrary"), vmem_limit_bytes=VMEM_LIMIT_BYTES),
        name="sb_attn",
    )(qkv, qkv, qkv, qkv, tri, bias)


def _mla_kernel(qa_ref, qb_ref, k_ref, v_ref, bias_ref, oa_ref, ob_ref,
                vaug_ref, s_ref, mx_ref, acc_ref):
    pq = pl.program_id(2)
    t = ATT_T
    nq = k_ref.shape[1] // t
    q_refs = (qa_ref, qb_ref)

    @pl.when(pq == 0)
    def _():
        vaug_ref[:, 0:LANES] = v_ref[0]
        vaug_ref[:, LANES:2 * LANES] = jnp.ones((v_ref.shape[1], LANES), BF16)

    mx_ref[...] = jnp.full(mx_ref.shape, MASK_BIAS, F32)
    acc_ref[...] = jnp.zeros(acc_ref.shape, F32)

    def scores(sel, j, slot, diag):
        s = _nt_dot(q_refs[sel][0], k_ref[0, _rows(j, t), :])
        if diag:
            s = s + bias_ref[...]
        s_ref[slot] = s
        colmax = s[:, 0:LANES]
        for cb in range(1, t // LANES):
            colmax = jnp.maximum(colmax, s[:, cb * LANES:(cb + 1) * LANES])
        mx_ref[sel] = jnp.maximum(mx_ref[sel], colmax)

    def accumulate(sel, j, slot):
        m = mx_ref[sel]
        p = jnp.exp2(s_ref[slot] - jnp.concatenate([m] * (t // LANES), axis=1)).astype(BF16)
        acc_ref[sel] += _dot(p, vaug_ref[_rows(j, t), :])

    def step(c):
        for sel, n_off, base in _groups(c, nq):
            _static_loop(n_off, lambda j, sel=sel, base=base: scores(sel, j, base + j, False))
            scores(sel, n_off, base + n_off, True)
            mx_ref[sel] = jnp.broadcast_to(jnp.max(mx_ref[sel], axis=-1, keepdims=True), (t, LANES))
        for sel, n_off, base in _groups(c, nq):
            _static_loop(n_off + 1, lambda j, sel=sel, base=base: accumulate(sel, j, base + j))

    _specialize(pq, nq // 2, step)

    for sel, o_ref in enumerate((oa_ref, ob_ref)):
        acc = acc_ref[sel]
        o_ref[0] = (acc[:, 0:LANES] / acc[:, LANES:2 * LANES]).astype(BF16)


def _mla_attention(qm, km, vm, bias):
    b, s, _ = qm.shape
    t = ATT_T
    nq = s // t
    const = lambda shape: pl.BlockSpec(shape, lambda bi, h, i: (0,) * len(shape), pipeline_mode=pl.Buffered(1))
    half = jax.ShapeDtypeStruct((b, s // 2, MLA_WIDTH), BF16)
    return pl.pallas_call(
        _mla_kernel,
        grid=(b, MLA_HEADS, nq // 2),
        in_specs=[
            pl.BlockSpec((1, t, _Q_HEAD_W), lambda bi, h, i: (bi, i, h)),
            pl.BlockSpec((1, t, _Q_HEAD_W), lambda bi, h, i: (bi, nq - 1 - i, h)),
            pl.BlockSpec((1, s, _Q_HEAD_W), lambda bi, h, i: (bi, 0, h)),
            pl.BlockSpec((1, s, LANES), lambda bi, h, i: (bi, 0, h)),
            const(bias.shape),
        ],
        out_specs=[
            pl.BlockSpec((1, t, LANES), lambda bi, h, i: (bi, i, h)),
            pl.BlockSpec((1, t, LANES), lambda bi, h, i: (bi, nq // 2 - 1 - i, h)),
        ],
        out_shape=[half, half],
        scratch_shapes=[
            pltpu.VMEM((s, 2 * LANES), BF16),
            pltpu.VMEM((nq + 1, t, t), F32),
            pltpu.VMEM((2, t, LANES), F32),
            pltpu.VMEM((2, t, 2 * LANES), F32),
        ],
        compiler_params=pltpu.CompilerParams(
            dimension_semantics=("arbitrary", "arbitrary", "arbitrary"), vmem_limit_bytes=VMEM_LIMIT_BYTES),
        name="mla_attn",
    )(qm, qm, km, vm, bias)


def _outffn_kernel(x_ref, osb_lo_ref, osb_hi_ref, omla_lo_ref, omla_hi_ref, mod_ref, gsb_ref, gmla_ref, gffn_ref,
                   wo_ref, wg_ref, wu_ref, wd_ref, out_ref, act_ref):
    low = pl.program_id(1) < pl.num_programs(1) // 2
    gate1 = mod_ref[0, 2:3, :]
    shift2 = mod_ref[0, 3:4, :]
    scale2 = mod_ref[0, 4:5, :]
    gate2 = mod_ref[0, 5:6, :]

    osb = jnp.where(low, osb_lo_ref[0], osb_hi_ref[0]).astype(F32)
    omla = jnp.where(low, omla_lo_ref[0], omla_hi_ref[0]).astype(F32)
    nsb = osb * lax.rsqrt(_mean_sq(osb, SB_WIDTH) + EPS) * gsb_ref[...]
    nmla = omla * lax.rsqrt(_mean_sq(omla, MLA_WIDTH) + EPS) * gmla_ref[...]
    mixed = jnp.concatenate([nsb.astype(BF16), nmla.astype(BF16)], axis=1)
    x1 = x_ref[0] + gate1 * _dot(mixed, wo_ref[...])

    h2 = (x1 * lax.rsqrt(_mean_sq(x1, D_MODEL) + EPS)) * (gffn_ref[...] * (1.0 + scale2)) + shift2
    h2 = h2.astype(BF16)
    d_ff = wg_ref.shape[1]
    for cidx in range(d_ff // FFN_TN):
        sl = slice(cidx * FFN_TN, (cidx + 1) * FFN_TN)
        g = _dot(h2, wg_ref[:, sl])
        u = _dot(h2, wu_ref[:, sl])
        act_ref[:, sl] = (g / (1.0 + jnp.exp(-g)) * u).astype(BF16)
    out_ref[0] = x1 + gate2 * _dot(act_ref[...], wd_ref[...])


def _outffn(x, osb, omla, mod3, gsb, gmla, gffn, wo, wg, wu, wd):
    b, s, d = x.shape
    tm = FFN_TM
    nt = s // tm
    d_ff = wg.shape[1]
    const = lambda shape: pl.BlockSpec(shape, lambda i, j: (0,) * len(shape), pipeline_mode=pl.Buffered(1))
    tok = lambda w: pl.BlockSpec((1, tm, w), lambda i, j: (i, j, 0))
    lo = lambda w: pl.BlockSpec((1, tm, w), lambda i, j: (i, jnp.minimum(j, nt // 2 - 1), 0))
    hi = lambda w: pl.BlockSpec((1, tm, w), lambda i, j: (i, jnp.maximum(j - nt // 2, 0), 0))
    return pl.pallas_call(
        _outffn_kernel,
        grid=(b, nt),
        in_specs=[
            tok(d), lo(SB_WIDTH), hi(SB_WIDTH), lo(MLA_WIDTH), hi(MLA_WIDTH),
            pl.BlockSpec((1, N_MOD, d), lambda i, j: (i, 0, 0)),
            const((1, SB_WIDTH)), const((1, MLA_WIDTH)), const((1, d)),
            const(wo.shape), const(wg.shape), const(wu.shape), const(wd.shape),
        ],
        out_specs=tok(d),
        out_shape=jax.ShapeDtypeStruct((b, s, d), F32),
        scratch_shapes=[pltpu.VMEM((tm, d_ff), BF16)],
        compiler_params=pltpu.CompilerParams(
            dimension_semantics=("arbitrary", "arbitrary"), vmem_limit_bytes=VMEM_LIMIT_BYTES),
        name="outffn",
    )(x, osb[0], osb[1], omla[0], omla[1], mod3, gsb, gmla, gffn, wo, wg, wu, wd)


def _swap_halves(n):
    half = n // 2
    return np.concatenate([np.arange(half, n), np.arange(0, half)])


def _layer_params(l, w_in, w_q_up, w_kv_up, q_norm, k_rope_norm):
    swap = _swap_halves(MLA_ROPE_DIM)
    kr0 = _C_CKV + MLA_KV_RANK
    kr = kr0 + np.arange(MLA_ROPE_DIM)
    krs = kr0 + swap
    w1_idx = np.concatenate([np.arange(kr0), kr, kr, krs, krs])
    w1 = w_in[l][:, w1_idx].astype(BF16)

    q_idx = []
    for hd in range(MLA_HEADS):
        base = hd * MLA_QK_DIM
        q_idx += [base + np.arange(MLA_NOPE_DIM), base + MLA_NOPE_DIM + np.arange(MLA_ROPE_DIM),
                  base + MLA_NOPE_DIM + swap]
    wq = w_q_up[l][:, np.concatenate(q_idx)].astype(BF16)

    per_head = MLA_NOPE_DIM + MLA_V_DIM
    kv_idx = ([hd * per_head + np.arange(MLA_NOPE_DIM) for hd in range(MLA_HEADS)]
              + [hd * per_head + MLA_NOPE_DIM + np.arange(MLA_V_DIM) for hd in range(MLA_HEADS)])
    wkv = w_kv_up[l][:, np.concatenate(kv_idx)].astype(BF16)

    gqn = q_norm[l][:MLA_NOPE_DIM].reshape(1, LANES)
    gr = q_norm[l][MLA_NOPE_DIM:]
    gqr = jnp.concatenate([gr, gr[swap]]).reshape(1, LANES)
    gk = k_rope_norm[l]
    gk1 = jnp.concatenate([gk, gk]).reshape(1, LANES)
    gk2 = jnp.concatenate([gk[swap], gk[swap]]).reshape(1, LANES)
    return w1, wq, wkv, gqn, gqr, gk1, gk2


def _constants():
    half = MLA_ROPE_DIM // 2
    freqs = 1.0 / (ROPE_THETA ** (jnp.arange(half, dtype=F32) / half))
    freq = jnp.tile(freqs, LANES // half).reshape(1, LANES)
    sign = np.tile(np.concatenate([-np.ones(half), np.ones(half)]), LANES // (2 * half))
    sign = jnp.asarray(sign, F32).reshape(1, LANES)
    t = ATT_T
    row = np.arange(t)[:, None]
    col = np.arange(t)[None, :]
    tri = (row >= col).astype(np.float32)
    tri = jnp.asarray(np.concatenate([tri, tri], axis=0), BF16)
    bias_strict = jnp.asarray(np.where(col < row, 0.0, MASK_BIAS), F32)
    bias_causal = jnp.asarray(np.where(col <= row, 0.0, MASK_BIAS), F32)
    return freq, sign, tri, bias_strict, bias_causal


def kernel(x, c, positions, w_ada, b_ada, norm_attn, norm_ffn, w_in, q_a_norm, w_q_up, kv_a_norm, w_kv_up,
           q_norm, k_nope_norm, k_rope_norm, out_norm_sb, out_norm_mla, w_out, w_gate, w_up, w_down):
    b, s, d = x.shape
    depth = w_ada.shape[0]
    freq, sign, tri, bias_strict, bias_causal = _constants()
    pos3 = positions.reshape(b, s, 1)
    for l in range(depth):
        mod3 = _adaln(c, w_ada[l], b_ada[l]).reshape(b, N_MOD, d)
        w1, wq, wkv, gqn, gqr, gk1, gk2 = _layer_params(l, w_in, w_q_up, w_kv_up, q_norm, k_rope_norm)
        qkv, qm, km, vm = _inproj(
            x, pos3, mod3, norm_attn[l].reshape(1, d), w1, q_a_norm[l].reshape(1, -1), wq,
            kv_a_norm[l].reshape(1, -1), wkv, gqn, gqr, k_nope_norm[l].reshape(1, LANES), gk1, gk2, freq, sign)
        osb = _sb_attention(qkv, tri, bias_strict)
        omla = _mla_attention(qm, km, vm, bias_causal)
        x = _outffn(x, osb, omla, mod3, out_norm_sb[l].reshape(1, -1), out_norm_mla[l].reshape(1, -1),
                    norm_ffn[l].reshape(1, d), w_out[l].astype(BF16), w_gate[l].astype(BF16),
                    w_up[l].astype(BF16), w_down[l].astype(BF16))
    return x
```

```python
import numpy as np
import jax
import jax.numpy as jnp
from jax import lax
from jax.experimental import pallas as pl
from jax.experimental.pallas import tpu as pltpu

F32 = jnp.float32
BF16 = jnp.bfloat16

D_MODEL = 1024
SB_HEADS = 8
SB_HEAD_DIM = 64
SB_WIDTH = SB_HEADS * SB_HEAD_DIM
MLA_HEADS = 4
MLA_NOPE_DIM = 128
MLA_ROPE_DIM = 64
MLA_QK_DIM = MLA_NOPE_DIM + MLA_ROPE_DIM
MLA_V_DIM = 128
MLA_Q_RANK = 384
MLA_KV_RANK = 256
MLA_WIDTH = MLA_HEADS * MLA_V_DIM
N_MOD = 6
ROPE_THETA = 10000.0
EPS = 1e-6
LOG2E = float(np.log2(np.e))

LANES = 128
VMEM_LIMIT_BYTES = 56 * 1024 * 1024

ADALN_TN = 1024
PROJ_TS = 512
ATT_T = 256
SB_UNROLL = 8
MLA_UNROLL = 8
FFN_TM = 512
FFN_TN = 256
MASK_BIAS = -1e9

_C_QSB = 0
_C_KSB = SB_WIDTH
_C_VSB = 2 * SB_WIDTH
_C_CQ = 3 * SB_WIDTH
_C_CKV = _C_CQ + MLA_Q_RANK
_C_KR = _C_CKV + MLA_KV_RANK
_C_KRS = _C_KR + LANES
_N_PROJ = _C_KRS + LANES
_Q_HEAD_W = 2 * LANES


def _nt_dot(a, b):
    return lax.dot_general(a, b, (((1,), (1,)), ((), ())), preferred_element_type=F32)


def _dot(a, b):
    return jnp.dot(a, b, preferred_element_type=F32)


def _mean_sq(x, n):
    return jnp.sum(x * x, axis=-1, keepdims=True) * (1.0 / n)


def _adaln_kernel(c_ref, w_ref, b_ref, o_ref):
    c = c_ref[...]
    sc = c / (1.0 + jnp.exp(-c))
    w = w_ref[...]
    s_hi = sc.astype(BF16)
    s_lo = (sc - s_hi.astype(F32)).astype(BF16)
    w_hi = w.astype(BF16)
    w_lo = (w - w_hi.astype(F32)).astype(BF16)
    acc = _dot(s_hi, w_hi) + _dot(s_lo, w_hi) + _dot(s_hi, w_lo)
    o_ref[...] = acc + b_ref[...]


def _adaln(c, w_ada, b_ada):
    b, d = c.shape
    n = w_ada.shape[1]
    return pl.pallas_call(
        _adaln_kernel,
        grid=(n // ADALN_TN,),
        in_specs=[
            pl.BlockSpec((b, d), lambda j: (0, 0)),
            pl.BlockSpec((d, ADALN_TN), lambda j: (0, j)),
            pl.BlockSpec((1, ADALN_TN), lambda j: (0, j)),
        ],
        out_specs=pl.BlockSpec((b, ADALN_TN), lambda j: (0, j)),
        out_shape=jax.ShapeDtypeStruct((b, n), F32),
        compiler_params=pltpu.CompilerParams(
            dimension_semantics=("arbitrary",), vmem_limit_bytes=VMEM_LIMIT_BYTES),
        name="adaln",
    )(c, w_ada, b_ada.reshape(1, n))


def _inproj_kernel(x_ref, pos_ref, mod_ref, gattn_ref, w1_ref, gqa_ref, wq_ref, gkva_ref, wkv_ref,
                   gqn_ref, gqr_ref, gkn_ref, gk1_ref, gk2_ref, freq_ref, sign_ref,
                   qkv_ref, qm_ref, km_ref, vm_ref):
    x = x_ref[0]
    shift1 = mod_ref[0, 0:1, :]
    scale1 = mod_ref[0, 1:2, :]
    r = lax.rsqrt(_mean_sq(x, D_MODEL) + EPS)
    h = (x * r) * (gattn_ref[...] * (1.0 + scale1)) + shift1
    proj = _dot(h.astype(BF16), w1_ref[...])

    qkv_ref[0, :, 0:SB_WIDTH] = (proj[:, _C_QSB:_C_KSB] * (SB_HEAD_DIM ** -0.5 * LOG2E)).astype(BF16)
    qkv_ref[0, :, SB_WIDTH:3 * SB_WIDTH] = proj[:, _C_KSB:_C_CQ].astype(BF16)

    pos = pos_ref[0].astype(F32)
    ang = pos * freq_ref[...]
    cos_t = jnp.cos(ang)
    sin_t = jnp.sin(ang) * sign_ref[...]

    cq = proj[:, _C_CQ:_C_CKV]
    cqn = cq * lax.rsqrt(_mean_sq(cq, MLA_Q_RANK) + EPS) * gqa_ref[...]
    y = _dot(cqn.astype(BF16), wq_ref[...])
    lane = lax.broadcasted_iota(jnp.int32, (1, LANES), 1)
    rope_q = jnp.where(lane < MLA_ROPE_DIM, cos_t, sin_t) * gqr_ref[...]
    for hd in range(MLA_HEADS):
        yn = y[:, hd * _Q_HEAD_W: hd * _Q_HEAD_W + LANES]
        yr = y[:, hd * _Q_HEAD_W + LANES: (hd + 1) * _Q_HEAD_W]
        ssq = jnp.sum(yn * yn, axis=-1, keepdims=True) + 0.5 * jnp.sum(yr * yr, axis=-1, keepdims=True)
        rq = lax.rsqrt(ssq * (1.0 / MLA_QK_DIM) + EPS) * (MLA_QK_DIM ** -0.5 * LOG2E)
        qm_ref[0, :, hd * _Q_HEAD_W: hd * _Q_HEAD_W + LANES] = (yn * rq * gqn_ref[...]).astype(BF16)
        qm_ref[0, :, hd * _Q_HEAD_W + LANES: (hd + 1) * _Q_HEAD_W] = (yr * rq * rope_q).astype(BF16)

    z1 = proj[:, _C_KR:_C_KRS]
    z2 = proj[:, _C_KRS:_N_PROJ]
    rk = lax.rsqrt(_mean_sq(z1, LANES) + EPS)
    k_rope = (rk * (z1 * (gk1_ref[...] * cos_t) + z2 * (gk2_ref[...] * sin_t))).astype(BF16)

    ckv = proj[:, _C_CKV:_C_KR]
    ckvn = ckv * lax.rsqrt(_mean_sq(ckv, MLA_KV_RANK) + EPS) * gkva_ref[...]
    kv = _dot(ckvn.astype(BF16), wkv_ref[...])
    for hd in range(MLA_HEADS):
        kn = kv[:, hd * LANES:(hd + 1) * LANES]
        km_ref[0, :, hd * _Q_HEAD_W: hd * _Q_HEAD_W + LANES] = (
            kn * lax.rsqrt(_mean_sq(kn, MLA_NOPE_DIM) + EPS) * gkn_ref[...]).astype(BF16)
        km_ref[0, :, hd * _Q_HEAD_W + LANES: (hd + 1) * _Q_HEAD_W] = k_rope
    vm_ref[0] = kv[:, MLA_WIDTH:].astype(BF16)


def _inproj(x, pos3, mod3, gattn, w1, gqa, wq, gkva, wkv, gqn, gqr, gkn, gk1, gk2, freq, sign):
    b, s, d = x.shape
    ts = PROJ_TS
    const = lambda shape: pl.BlockSpec(shape, lambda i, j: (0,) * len(shape), pipeline_mode=pl.Buffered(1))
    tok = lambda w: pl.BlockSpec((1, ts, w), lambda i, j: (i, j, 0))
    return pl.pallas_call(
        _inproj_kernel,
        grid=(b, s // ts),
        in_specs=[
            tok(d),
            tok(1),
            pl.BlockSpec((1, N_MOD, d), lambda i, j: (i, 0, 0)),
            const((1, d)),
            const(w1.shape),
            const((1, MLA_Q_RANK)),
            const(wq.shape),
            const((1, MLA_KV_RANK)),
            const(wkv.shape),
            const((1, LANES)), const((1, LANES)), const((1, LANES)), const((1, LANES)), const((1, LANES)),
            const((1, LANES)), const((1, LANES)),
        ],
        out_specs=[tok(3 * SB_WIDTH), tok(MLA_HEADS * _Q_HEAD_W), tok(MLA_HEADS * _Q_HEAD_W), tok(MLA_WIDTH)],
        out_shape=[
            jax.ShapeDtypeStruct((b, s, 3 * SB_WIDTH), BF16),
            jax.ShapeDtypeStruct((b, s, MLA_HEADS * _Q_HEAD_W), BF16),
            jax.ShapeDtypeStruct((b, s, MLA_HEADS * _Q_HEAD_W), BF16),
            jax.ShapeDtypeStruct((b, s, MLA_WIDTH), BF16),
        ],
        compiler_params=pltpu.CompilerParams(
            dimension_semantics=("arbitrary", "arbitrary"), vmem_limit_bytes=VMEM_LIMIT_BYTES),
        name="inproj",
    )(x, pos3, mod3, gattn, w1, gqa, wq, gkva, wkv, gqn, gqr, gkn, gk1, gk2, freq, sign)


def _groups(c, nq):
    return ((0, c, 0), (1, nq - 1 - c, c + 1))


def _static_loop(n, body, unroll):
    if n <= unroll:
        for j in range(n):
            body(j)
    else:
        def step(j, carry):
            body(j)
            return carry
        lax.fori_loop(0, n, step, 0, unroll=unroll)


def _neg_abs(x):
    bits = lax.bitcast_convert_type(x, jnp.uint32) | jnp.uint32(0x80000000)
    return lax.bitcast_convert_type(bits, F32)


def _rows(j, t):
    if isinstance(j, int):
        return pl.ds(j * t, t)
    return pl.ds(pl.multiple_of(j * t, t), t)


def _specialize(step_id, n_steps, body):
    for c in range(n_steps):
        pl.when(step_id == c)(lambda c=c: body(c))


def _sb_kernel(qa_ref, qb_ref, k_ref, v_ref, tri_ref, bias_ref, oa_ref, ob_ref,
               qs_ref, z_ref, sp_ref, w_ref, f_ref, acc_ref):
    t = ATT_T
    nq = k_ref.shape[1] // t
    lane = lax.broadcasted_iota(jnp.int32, (1, LANES), 1)
    first = lane < SB_HEAD_DIM

    for sel, q_ref in enumerate((qa_ref, qb_ref)):
        q = q_ref[0]
        qs_ref[2 * sel] = jnp.where(first, q, jnp.zeros_like(q))
        qs_ref[2 * sel + 1] = jnp.where(first, jnp.zeros_like(q), q)
    acc_ref[...] = jnp.zeros(acc_ref.shape, F32)

    def scores(sel, j, slot, diag):
        k = k_ref[0, _rows(j, t), :]
        for hh in range(2):
            z = _nt_dot(qs_ref[2 * sel + hh], k)
            if diag:
                z = z + bias_ref[...]
            sp = jnp.maximum(z, 0.0) + jnp.log(1.0 + jnp.exp2(_neg_abs(z))) * LOG2E
            z_ref[slot, hh] = z
            sp_ref[slot, hh] = sp.astype(BF16)

    def weights(slot):
        for hh in range(2):
            cin = _dot(sp_ref[slot, hh], tri_ref[...])
            w_ref[slot, hh] = jnp.exp2(z_ref[slot, hh] - cin).astype(BF16)
            f_ref[slot, hh] = jnp.exp2(-jnp.broadcast_to(cin[:, 0:1], (t, LANES)))

    def accumulate(sel, j, slot):
        v = v_ref[0, _rows(j, t), :]
        for hh in range(2):
            acc_ref[sel, hh] = acc_ref[sel, hh] * f_ref[slot, hh] + _dot(w_ref[slot, hh], v)

    def step(c):
        for sel, n_off, base in _groups(c, nq):
            _static_loop(n_off, lambda j, sel=sel, base=base: scores(sel, j, base + j, False), SB_UNROLL)
            scores(sel, n_off, base + n_off, True)
        _static_loop(nq + 1, weights, SB_UNROLL)
        for sel, n_off, base in _groups(c, nq):
            _static_loop(n_off + 1, lambda j, sel=sel, base=base: accumulate(sel, j, base + j), SB_UNROLL)

    _specialize(pl.program_id(2), nq // 2, step)

    oa_ref[0] = jnp.where(first, acc_ref[0, 0], acc_ref[0, 1]).astype(BF16)
    ob_ref[0] = jnp.where(first, acc_ref[1, 0], acc_ref[1, 1]).astype(BF16)


def _sb_attention(qkv, tri, bias):
    b, s, _ = qkv.shape
    t = ATT_T
    nq = s // t
    n_pairs = SB_WIDTH // LANES
    const = lambda shape: pl.BlockSpec(shape, lambda bi, p, i: (0,) * len(shape), pipeline_mode=pl.Buffered(1))
    half = jax.ShapeDtypeStruct((b, s // 2, SB_WIDTH), BF16)
    return pl.pallas_call(
        _sb_kernel,
        grid=(b, n_pairs, nq // 2),
        in_specs=[
            pl.BlockSpec((1, t, LANES), lambda bi, p, i: (bi, i, p)),
            pl.BlockSpec((1, t, LANES), lambda bi, p, i: (bi, nq - 1 - i, p)),
            pl.BlockSpec((1, s, LANES), lambda bi, p, i: (bi, 0, n_pairs + p)),
            pl.BlockSpec((1, s, LANES), lambda bi, p, i: (bi, 0, 2 * n_pairs + p)),
            const(tri.shape),
            const(bias.shape),
        ],
        out_specs=[
            pl.BlockSpec((1, t, LANES), lambda bi, p, i: (bi, i, p)),
            pl.BlockSpec((1, t, LANES), lambda bi, p, i: (bi, nq // 2 - 1 - i, p)),
        ],
        out_shape=[half, half],
        scratch_shapes=[
            pltpu.VMEM((4, t, LANES), BF16),
            pltpu.VMEM((nq + 1, 2, t, t), F32),
            pltpu.VMEM((nq + 1, 2, t, t), BF16),
            pltpu.VMEM((nq + 1, 2, t, t), BF16),
            pltpu.VMEM((nq + 1, 2, t, LANES), F32),
            pltpu.VMEM((2, 2, t, LANES), F32),
        ],
        compiler_params=pltpu.CompilerParams(
            dimension_semantics=("arbitrary", "arbitrary", "arbitrary"), vmem_limit_bytes=VMEM_LIMIT_BYTES),
        name="sb_attn",
    )(qkv, qkv, qkv, qkv, tri, bias)


def _mla_kernel(qa_ref, qb_ref, k_ref, v_ref, bias_ref, oa_ref, ob_ref,
                vaug_ref, s_ref, mx_ref, acc_ref):
    pq = pl.program_id(2)
    t = ATT_T
    nq = k_ref.shape[1] // t
    q_refs = (qa_ref, qb_ref)

    @pl.when(pq == 0)
    def _():
        vaug_ref[:, 0:LANES] = v_ref[0]
        vaug_ref[:, LANES:2 * LANES] = jnp.ones((v_ref.shape[1], LANES), BF16)

    mx_ref[...] = jnp.full(mx_ref.shape, MASK_BIAS, F32)
    acc_ref[...] = jnp.zeros(acc_ref.shape, F32)

    def scores(sel, j, slot, diag):
        s = _nt_dot(q_refs[sel][0], k_ref[0, _rows(j, t), :])
        if diag:
            s = s + bias_ref[...]
        s_ref[slot] = s
        colmax = s[:, 0:LANES]
        for cb in range(1, t // LANES):
            colmax = jnp.maximum(colmax, s[:, cb * LANES:(cb + 1) * LANES])
        mx_ref[sel] = jnp.maximum(mx_ref[sel], colmax)

    def accumulate(sel, j, slot):
        m = mx_ref[sel]
        p = jnp.exp2(s_ref[slot] - jnp.concatenate([m] * (t // LANES), axis=1)).astype(BF16)
        acc_ref[sel] += _dot(p, vaug_ref[_rows(j, t), :])

    def step(c):
        for sel, n_off, base in _groups(c, nq):
            _static_loop(n_off, lambda j, sel=sel, base=base: scores(sel, j, base + j, False), MLA_UNROLL)
            scores(sel, n_off, base + n_off, True)
            mx_ref[sel] = jnp.broadcast_to(jnp.max(mx_ref[sel], axis=-1, keepdims=True), (t, LANES))
        for sel, n_off, base in _groups(c, nq):
            _static_loop(n_off + 1, lambda j, sel=sel, base=base: accumulate(sel, j, base + j), MLA_UNROLL)

    _specialize(pq, nq // 2, step)

    for sel, o_ref in enumerate((oa_ref, ob_ref)):
        acc = acc_ref[sel]
        o_ref[0] = (acc[:, 0:LANES] / acc[:, LANES:2 * LANES]).astype(BF16)


def _mla_attention(qm, km, vm, bias):
    b, s, _ = qm.shape
    t = ATT_T
    nq = s // t
    const = lambda shape: pl.BlockSpec(shape, lambda bi, h, i: (0,) * len(shape), pipeline_mode=pl.Buffered(1))
    half = jax.ShapeDtypeStruct((b, s // 2, MLA_WIDTH), BF16)
    return pl.pallas_call(
        _mla_kernel,
        grid=(b, MLA_HEADS, nq // 2),
        in_specs=[
            pl.BlockSpec((1, t, _Q_HEAD_W), lambda bi, h, i: (bi, i, h)),
            pl.BlockSpec((1, t, _Q_HEAD_W), lambda bi, h, i: (bi, nq - 1 - i, h)),
            pl.BlockSpec((1, s, _Q_HEAD_W), lambda bi, h, i: (bi, 0, h)),
            pl.BlockSpec((1, s, LANES), lambda bi, h, i: (bi, 0, h)),
            const(bias.shape),
        ],
        out_specs=[
            pl.BlockSpec((1, t, LANES), lambda bi, h, i: (bi, i, h)),
            pl.BlockSpec((1, t, LANES), lambda bi, h, i: (bi, nq // 2 - 1 - i, h)),
        ],
        out_shape=[half, half],
        scratch_shapes=[
            pltpu.VMEM((s, 2 * LANES), BF16),
            pltpu.VMEM((nq + 1, t, t), F32),
            pltpu.VMEM((2, t, LANES), F32),
            pltpu.VMEM((2, t, 2 * LANES), F32),
        ],
        compiler_params=pltpu.CompilerParams(
            dimension_semantics=("arbitrary", "arbitrary", "arbitrary"), vmem_limit_bytes=VMEM_LIMIT_BYTES),
        name="mla_attn",
    )(qm, qm, km, vm, bias)


def _outffn_kernel(x_ref, osb_lo_ref, osb_hi_ref, omla_lo_ref, omla_hi_ref, mod_ref, gsb_ref, gmla_ref, gffn_ref,
                   wo_ref, wg_ref, wu_ref, wd_ref, out_ref, act_ref):
    low = pl.program_id(1) < pl.num_programs(1) // 2
    gate1 = mod_ref[0, 2:3, :]
    shift2 = mod_ref[0, 3:4, :]
    scale2 = mod_ref[0, 4:5, :]
    gate2 = mod_ref[0, 5:6, :]

    osb = jnp.where(low, osb_lo_ref[0], osb_hi_ref[0]).astype(F32)
    omla = jnp.where(low, omla_lo_ref[0], omla_hi_ref[0]).astype(F32)
    nsb = osb * lax.rsqrt(_mean_sq(osb, SB_WIDTH) + EPS) * gsb_ref[...]
    nmla = omla * lax.rsqrt(_mean_sq(omla, MLA_WIDTH) + EPS) * gmla_ref[...]
    mixed = jnp.concatenate([nsb.astype(BF16), nmla.astype(BF16)], axis=1)
    x1 = x_ref[0] + gate1 * _dot(mixed, wo_ref[...])

    h2 = (x1 * lax.rsqrt(_mean_sq(x1, D_MODEL) + EPS)) * (gffn_ref[...] * (1.0 + scale2)) + shift2
    h2 = h2.astype(BF16)
    d_ff = wg_ref.shape[1]
    for cidx in range(d_ff // FFN_TN):
        sl = slice(cidx * FFN_TN, (cidx + 1) * FFN_TN)
        g = _dot(h2, wg_ref[:, sl])
        u = _dot(h2, wu_ref[:, sl])
        act_ref[:, sl] = (g / (1.0 + jnp.exp(-g)) * u).astype(BF16)
    out_ref[0] = x1 + gate2 * _dot(act_ref[...], wd_ref[...])


def _outffn(x, osb, omla, mod3, gsb, gmla, gffn, wo, wg, wu, wd):
    b, s, d = x.shape
    tm = FFN_TM
    nt = s // tm
    d_ff = wg.shape[1]
    const = lambda shape: pl.BlockSpec(shape, lambda i, j: (0,) * len(shape), pipeline_mode=pl.Buffered(1))
    tok = lambda w: pl.BlockSpec((1, tm, w), lambda i, j: (i, j, 0))
    lo = lambda w: pl.BlockSpec((1, tm, w), lambda i, j: (i, jnp.minimum(j, nt // 2 - 1), 0))
    hi = lambda w: pl.BlockSpec((1, tm, w), lambda i, j: (i, jnp.maximum(j - nt // 2, 0), 0))
    return pl.pallas_call(
        _outffn_kernel,
        grid=(b, nt),
        in_specs=[
            tok(d), lo(SB_WIDTH), hi(SB_WIDTH), lo(MLA_WIDTH), hi(MLA_WIDTH),
            pl.BlockSpec((1, N_MOD, d), lambda i, j: (i, 0, 0)),
            const((1, SB_WIDTH)), const((1, MLA_WIDTH)), const((1, d)),
            const(wo.shape), const(wg.shape), const(wu.shape), const(wd.shape),
        ],
        out_specs=tok(d),
        out_shape=jax.ShapeDtypeStruct((b, s, d), F32),
        scratch_shapes=[pltpu.VMEM((tm, d_ff), BF16)],
        compiler_params=pltpu.CompilerParams(
            dimension_semantics=("arbitrary", "arbitrary"), vmem_limit_bytes=VMEM_LIMIT_BYTES),
        name="outffn",
    )(x, osb[0], osb[1], omla[0], omla[1], mod3, gsb, gmla, gffn, wo, wg, wu, wd)


def _swap_halves(n):
    half = n // 2
    return np.concatenate([np.arange(half, n), np.arange(0, half)])


def _layer_params(l, w_in, w_q_up, w_kv_up, q_norm, k_rope_norm):
    swap = _swap_halves(MLA_ROPE_DIM)
    kr0 = _C_CKV + MLA_KV_RANK
    kr = kr0 + np.arange(MLA_ROPE_DIM)
    krs = kr0 + swap
    w1_idx = np.concatenate([np.arange(kr0), kr, kr, krs, krs])
    w1 = w_in[l][:, w1_idx].astype(BF16)

    q_idx = []
    for hd in range(MLA_HEADS):
        base = hd * MLA_QK_DIM
        q_idx += [base + np.arange(MLA_NOPE_DIM), base + MLA_NOPE_DIM + np.arange(MLA_ROPE_DIM),
                  base + MLA_NOPE_DIM + swap]
    wq = w_q_up[l][:, np.concatenate(q_idx)].astype(BF16)

    per_head = MLA_NOPE_DIM + MLA_V_DIM
    kv_idx = ([hd * per_head + np.arange(MLA_NOPE_DIM) for hd in range(MLA_HEADS)]
              + [hd * per_head + MLA_NOPE_DIM + np.arange(MLA_V_DIM) for hd in range(MLA_HEADS)])
    wkv = w_kv_up[l][:, np.concatenate(kv_idx)].astype(BF16)

    gqn = q_norm[l][:MLA_NOPE_DIM].reshape(1, LANES)
    gr = q_norm[l][MLA_NOPE_DIM:]
    gqr = jnp.concatenate([gr, gr[swap]]).reshape(1, LANES)
    gk = k_rope_norm[l]
    gk1 = jnp.concatenate([gk, gk]).reshape(1, LANES)
    gk2 = jnp.concatenate([gk[swap], gk[swap]]).reshape(1, LANES)
    return w1, wq, wkv, gqn, gqr, gk1, gk2


def _constants():
    half = MLA_ROPE_DIM // 2
    freqs = 1.0 / (ROPE_THETA ** (jnp.arange(half, dtype=F32) / half))
    freq = jnp.tile(freqs, LANES // half).reshape(1, LANES)
    sign = np.tile(np.concatenate([-np.ones(half), np.ones(half)]), LANES // (2 * half))
    sign = jnp.asarray(sign, F32).reshape(1, LANES)
    t = ATT_T
    row = np.arange(t)[:, None]
    col = np.arange(t)[None, :]
    tri = jnp.asarray((row >= col).astype(np.float32), BF16)
    bias_strict = jnp.asarray(np.where(col < row, 0.0, MASK_BIAS), F32)
    bias_causal = jnp.asarray(np.where(col <= row, 0.0, MASK_BIAS), F32)
    return freq, sign, tri, bias_strict, bias_causal


def kernel(x, c, positions, w_ada, b_ada, norm_attn, norm_ffn, w_in, q_a_norm, w_q_up, kv_a_norm, w_kv_up,
           q_norm, k_nope_norm, k_rope_norm, out_norm_sb, out_norm_mla, w_out, w_gate, w_up, w_down):
    b, s, d = x.shape
    depth = w_ada.shape[0]
    freq, sign, tri, bias_strict, bias_causal = _constants()
    pos3 = positions.reshape(b, s, 1)
    for l in range(depth):
        mod3 = _adaln(c, w_ada[l], b_ada[l]).reshape(b, N_MOD, d)
        w1, wq, wkv, gqn, gqr, gk1, gk2 = _layer_params(l, w_in, w_q_up, w_kv_up, q_norm, k_rope_norm)
        qkv, qm, km, vm = _inproj(
            x, pos3, mod3, norm_attn[l].reshape(1, d), w1, q_a_norm[l].reshape(1, -1), wq,
            kv_a_norm[l].reshape(1, -1), wkv, gqn, gqr, k_nope_norm[l].reshape(1, LANES), gk1, gk2, freq, sign)
        osb = _sb_attention(qkv, tri, bias_strict)
        omla = _mla_attention(qm, km, vm, bias_causal)
        x = _outffn(x, osb, omla, mod3, out_norm_sb[l].reshape(1, -1), out_norm_mla[l].reshape(1, -1),
                    norm_ffn[l].reshape(1, d), w_out[l].astype(BF16), w_gate[l].astype(BF16),
                    w_up[l].astype(BF16), w_down[l].astype(BF16))
    return x
```

```python
import numpy as np
import jax
import jax.numpy as jnp
from jax import lax
from jax.experimental import pallas as pl
from jax.experimental.pallas import tpu as pltpu

F32 = jnp.float32
BF16 = jnp.bfloat16

D_MODEL = 1024
SB_HEADS = 8
SB_HEAD_DIM = 64
SB_WIDTH = SB_HEADS * SB_HEAD_DIM
MLA_HEADS = 4
MLA_NOPE_DIM = 128
MLA_ROPE_DIM = 64
MLA_QK_DIM = MLA_NOPE_DIM + MLA_ROPE_DIM
MLA_V_DIM = 128
MLA_Q_RANK = 384
MLA_KV_RANK = 256
MLA_WIDTH = MLA_HEADS * MLA_V_DIM
N_MOD = 6
ROPE_THETA = 10000.0
EPS = 1e-6
LOG2E = float(np.log2(np.e))

LANES = 128
VMEM_LIMIT_BYTES = 56 * 1024 * 1024

ADALN_TN = 1024
PROJ_TS = 512
ATT_T = 256
SB_UNROLL = 8
MLA_UNROLL = 8
FFN_TM = 512
FFN_TN = 256
MASK_BIAS = -1e9

_C_QSB = 0
_C_KSB = SB_WIDTH
_C_VSB = 2 * SB_WIDTH
_C_CQ = 3 * SB_WIDTH
_C_CKV = _C_CQ + MLA_Q_RANK
_C_KR = _C_CKV + MLA_KV_RANK
_C_KRS = _C_KR + LANES
_N_PROJ = _C_KRS + LANES
_Q_HEAD_W = 2 * LANES


def _nt_dot(a, b):
    return lax.dot_general(a, b, (((1,), (1,)), ((), ())), preferred_element_type=F32)


def _dot(a, b):
    return jnp.dot(a, b, preferred_element_type=F32)


def _mean_sq(x, n):
    return jnp.sum(x * x, axis=-1, keepdims=True) * (1.0 / n)


def _adaln_kernel(c_ref, w_ref, b_ref, o_ref):
    c = c_ref[...]
    sc = c / (1.0 + jnp.exp(-c))
    w = w_ref[...]
    s_hi = sc.astype(BF16)
    s_lo = (sc - s_hi.astype(F32)).astype(BF16)
    w_hi = w.astype(BF16)
    w_lo = (w - w_hi.astype(F32)).astype(BF16)
    acc = _dot(s_hi, w_hi) + _dot(s_lo, w_hi) + _dot(s_hi, w_lo)
    o_ref[...] = acc + b_ref[...]


def _adaln(c, w_ada, b_ada):
    b, d = c.shape
    n = w_ada.shape[1]
    return pl.pallas_call(
        _adaln_kernel,
        grid=(n // ADALN_TN,),
        in_specs=[
            pl.BlockSpec((b, d), lambda j: (0, 0)),
            pl.BlockSpec((d, ADALN_TN), lambda j: (0, j)),
            pl.BlockSpec((1, ADALN_TN), lambda j: (0, j)),
        ],
        out_specs=pl.BlockSpec((b, ADALN_TN), lambda j: (0, j)),
        out_shape=jax.ShapeDtypeStruct((b, n), F32),
        compiler_params=pltpu.CompilerParams(
            dimension_semantics=("arbitrary",), vmem_limit_bytes=VMEM_LIMIT_BYTES),
        name="adaln",
    )(c, w_ada, b_ada.reshape(1, n))


def _inproj_kernel(x_ref, pos_ref, mod_ref, gattn_ref, w1_ref, gqa_ref, wq_ref, gkva_ref, wkv_ref,
                   gqn_ref, gqr_ref, gkn_ref, gk1_ref, gk2_ref, freq_ref, sign_ref,
                   qkv_ref, qm_ref, km_ref, vm_ref):
    x = x_ref[0]
    shift1 = mod_ref[0, 0:1, :]
    scale1 = mod_ref[0, 1:2, :]
    r = lax.rsqrt(_mean_sq(x, D_MODEL) + EPS)
    h = (x * r) * (gattn_ref[...] * (1.0 + scale1)) + shift1
    proj = _dot(h.astype(BF16), w1_ref[...])

    qkv_ref[0, :, 0:SB_WIDTH] = (proj[:, _C_QSB:_C_KSB] * (SB_HEAD_DIM ** -0.5)).astype(BF16)
    qkv_ref[0, :, SB_WIDTH:3 * SB_WIDTH] = proj[:, _C_KSB:_C_CQ].astype(BF16)

    pos = pos_ref[0].astype(F32)
    ang = pos * freq_ref[...]
    cos_t = jnp.cos(ang)
    sin_t = jnp.sin(ang) * sign_ref[...]

    cq = proj[:, _C_CQ:_C_CKV]
    cqn = cq * lax.rsqrt(_mean_sq(cq, MLA_Q_RANK) + EPS) * gqa_ref[...]
    y = _dot(cqn.astype(BF16), wq_ref[...])
    lane = lax.broadcasted_iota(jnp.int32, (1, LANES), 1)
    rope_q = jnp.where(lane < MLA_ROPE_DIM, cos_t, sin_t) * gqr_ref[...]
    for hd in range(MLA_HEADS):
        yn = y[:, hd * _Q_HEAD_W: hd * _Q_HEAD_W + LANES]
        yr = y[:, hd * _Q_HEAD_W + LANES: (hd + 1) * _Q_HEAD_W]
        ssq = jnp.sum(yn * yn, axis=-1, keepdims=True) + 0.5 * jnp.sum(yr * yr, axis=-1, keepdims=True)
        rq = lax.rsqrt(ssq * (1.0 / MLA_QK_DIM) + EPS) * (MLA_QK_DIM ** -0.5 * LOG2E)
        qm_ref[0, :, hd * _Q_HEAD_W: hd * _Q_HEAD_W + LANES] = (yn * rq * gqn_ref[...]).astype(BF16)
        qm_ref[0, :, hd * _Q_HEAD_W + LANES: (hd + 1) * _Q_HEAD_W] = (yr * rq * rope_q).astype(BF16)

    z1 = proj[:, _C_KR:_C_KRS]
    z2 = proj[:, _C_KRS:_N_PROJ]
    rk = lax.rsqrt(_mean_sq(z1, LANES) + EPS)
    k_rope = (rk * (z1 * (gk1_ref[...] * cos_t) + z2 * (gk2_ref[...] * sin_t))).astype(BF16)

    ckv = proj[:, _C_CKV:_C_KR]
    ckvn = ckv * lax.rsqrt(_mean_sq(ckv, MLA_KV_RANK) + EPS) * gkva_ref[...]
    kv = _dot(ckvn.astype(BF16), wkv_ref[...])
    for hd in range(MLA_HEADS):
        kn = kv[:, hd * LANES:(hd + 1) * LANES]
        km_ref[0, :, hd * _Q_HEAD_W: hd * _Q_HEAD_W + LANES] = (
            kn * lax.rsqrt(_mean_sq(kn, MLA_NOPE_DIM) + EPS) * gkn_ref[...]).astype(BF16)
        km_ref[0, :, hd * _Q_HEAD_W + LANES: (hd + 1) * _Q_HEAD_W] = k_rope
    vm_ref[0] = kv[:, MLA_WIDTH:].astype(BF16)


def _inproj(x, pos3, mod3, gattn, w1, gqa, wq, gkva, wkv, gqn, gqr, gkn, gk1, gk2, freq, sign):
    b, s, d = x.shape
    ts = PROJ_TS
    const = lambda shape: pl.BlockSpec(shape, lambda i, j: (0,) * len(shape), pipeline_mode=pl.Buffered(1))
    tok = lambda w: pl.BlockSpec((1, ts, w), lambda i, j: (i, j, 0))
    return pl.pallas_call(
        _inproj_kernel,
        grid=(b, s // ts),
        in_specs=[
            tok(d),
            tok(1),
            pl.BlockSpec((1, N_MOD, d), lambda i, j: (i, 0, 0)),
            const((1, d)),
            const(w1.shape),
            const((1, MLA_Q_RANK)),
            const(wq.shape),
            const((1, MLA_KV_RANK)),
            const(wkv.shape),
            const((1, LANES)), const((1, LANES)), const((1, LANES)), const((1, LANES)), const((1, LANES)),
            const((1, LANES)), const((1, LANES)),
        ],
        out_specs=[tok(3 * SB_WIDTH), tok(MLA_HEADS * _Q_HEAD_W), tok(MLA_HEADS * _Q_HEAD_W), tok(MLA_WIDTH)],
        out_shape=[
            jax.ShapeDtypeStruct((b, s, 3 * SB_WIDTH), BF16),
            jax.ShapeDtypeStruct((b, s, MLA_HEADS * _Q_HEAD_W), BF16),
            jax.ShapeDtypeStruct((b, s, MLA_HEADS * _Q_HEAD_W), BF16),
            jax.ShapeDtypeStruct((b, s, MLA_WIDTH), BF16),
        ],
        compiler_params=pltpu.CompilerParams(
            dimension_semantics=("arbitrary", "arbitrary"), vmem_limit_bytes=VMEM_LIMIT_BYTES),
        name="inproj",
    )(x, pos3, mod3, gattn, w1, gqa, wq, gkva, wkv, gqn, gqr, gkn, gk1, gk2, freq, sign)


def _groups(c, nq):
    return ((0, c, 0), (1, nq - 1 - c, c + 1))


def _static_loop(n, body, unroll):
    if n <= unroll:
        for j in range(n):
            body(j)
    else:
        def step(j, carry):
            body(j)
            return carry
        lax.fori_loop(0, n, step, 0, unroll=unroll)


def _rows(j, t):
    if isinstance(j, int):
        return pl.ds(j * t, t)
    return pl.ds(pl.multiple_of(j * t, t), t)


def _sb_kernel(q_ref, k_ref, v_ref, tri_ref, bias_ref, o_ref):
    t = ATT_T
    nq = k_ref.shape[1] // t
    lane = lax.broadcasted_iota(jnp.int32, (1, LANES), 1)
    first = lane < SB_HEAD_DIM

    def q_block(qb):
        q = q_ref[0, _rows(qb, t), :]
        qh = (jnp.where(first, q, jnp.zeros_like(q)), jnp.where(first, jnp.zeros_like(q), q))
        z, sp = {}, {}
        for j in range(qb + 1):
            k = k_ref[0, _rows(j, t), :]
            for hh in range(2):
                zz = _nt_dot(qh[hh], k)
                if j == qb:
                    zz = zz + bias_ref[...]
                e = jnp.exp2(jnp.abs(zz) * (-LOG2E))
                z[j, hh] = zz
                sp[j, hh] = (jnp.maximum(zz, 0.0) + jnp.log(1.0 + e)).astype(BF16)
        outs = []
        for hh in range(2):
            run = jnp.zeros((t, LANES), F32)
            acc = jnp.zeros((t, LANES), F32)
            for j in range(qb, -1, -1):
                cin = _dot(sp[j, hh], tri_ref[...])
                arg = z[j, hh] - cin - jnp.concatenate([run] * (t // LANES), axis=1)
                w = jnp.exp2(arg * LOG2E).astype(BF16)
                acc = acc + _dot(w, v_ref[0, _rows(j, t), :])
                run = run + jnp.broadcast_to(cin[:, 0:1], (t, LANES))
            outs.append(acc)
        o_ref[0, _rows(qb, t), :] = jnp.where(first, outs[0], outs[1]).astype(BF16)

    for c in range(nq // 2):
        q_block(c)
        q_block(nq - 1 - c)


def _sb_attention(qkv, tri, bias):
    b, s, _ = qkv.shape
    t = ATT_T
    nq = s // t
    n_pairs = SB_WIDTH // LANES
    const = lambda shape: pl.BlockSpec(shape, lambda bi, p: (0,) * len(shape), pipeline_mode=pl.Buffered(1))
    return pl.pallas_call(
        _sb_kernel,
        grid=(b, n_pairs),
        in_specs=[
            pl.BlockSpec((1, s, LANES), lambda bi, p: (bi, 0, p)),
            pl.BlockSpec((1, s, LANES), lambda bi, p: (bi, 0, n_pairs + p)),
            pl.BlockSpec((1, s, LANES), lambda bi, p: (bi, 0, 2 * n_pairs + p)),
            const(tri.shape),
            const(bias.shape),
        ],
        out_specs=pl.BlockSpec((1, s, LANES), lambda bi, p: (bi, 0, p)),
        out_shape=jax.ShapeDtypeStruct((b, s, SB_WIDTH), BF16),
        compiler_params=pltpu.CompilerParams(
            dimension_semantics=("arbitrary", "arbitrary"), vmem_limit_bytes=VMEM_LIMIT_BYTES),
        name="sb_attn",
    )(qkv, qkv, qkv, tri, bias)


def _mla_kernel(q_ref, k_ref, v_ref, bias_ref, o_ref, vaug_ref, s_ref, mx_ref, acc_ref):
    t = ATT_T
    nq = k_ref.shape[1] // t

    vaug_ref[:, 0:LANES] = v_ref[0]
    vaug_ref[:, LANES:2 * LANES] = jnp.ones((v_ref.shape[1], LANES), BF16)

    def scores(sel, qb, j, slot, diag):
        s = _nt_dot(q_ref[0, _rows(qb, t), :], k_ref[0, _rows(j, t), :])
        if diag:
            s = s + bias_ref[...]
        s_ref[slot] = s
        colmax = s[:, 0:LANES]
        for cb in range(1, t // LANES):
            colmax = jnp.maximum(colmax, s[:, cb * LANES:(cb + 1) * LANES])
        mx_ref[sel] = jnp.maximum(mx_ref[sel], colmax)

    def accumulate(sel, j, slot):
        m = mx_ref[sel]
        p = jnp.exp2(s_ref[slot] - jnp.concatenate([m] * (t // LANES), axis=1)).astype(BF16)
        acc_ref[sel] += _dot(p, vaug_ref[_rows(j, t), :])

    for c in range(nq // 2):
        q_blocks = (c, nq - 1 - c)
        mx_ref[...] = jnp.full(mx_ref.shape, MASK_BIAS, F32)
        acc_ref[...] = jnp.zeros(acc_ref.shape, F32)
        for sel, n_off, base in _groups(c, nq):
            qb = q_blocks[sel]
            _static_loop(n_off, lambda j, sel=sel, qb=qb, base=base: scores(sel, qb, j, base + j, False),
                         MLA_UNROLL)
            scores(sel, qb, n_off, base + n_off, True)
            mx_ref[sel] = jnp.broadcast_to(jnp.max(mx_ref[sel], axis=-1, keepdims=True), (t, LANES))
        for sel, n_off, base in _groups(c, nq):
            _static_loop(n_off + 1, lambda j, sel=sel, base=base: accumulate(sel, j, base + j), MLA_UNROLL)
        for sel, qb in enumerate(q_blocks):
            acc = acc_ref[sel]
            o_ref[0, _rows(qb, t), :] = (acc[:, 0:LANES] / acc[:, LANES:2 * LANES]).astype(BF16)


def _mla_attention(qm, km, vm, bias):
    b, s, _ = qm.shape
    t = ATT_T
    nq = s // t
    const = lambda shape: pl.BlockSpec(shape, lambda bi, h: (0,) * len(shape), pipeline_mode=pl.Buffered(1))
    return pl.pallas_call(
        _mla_kernel,
        grid=(b, MLA_HEADS),
        in_specs=[
            pl.BlockSpec((1, s, _Q_HEAD_W), lambda bi, h: (bi, 0, h)),
            pl.BlockSpec((1, s, _Q_HEAD_W), lambda bi, h: (bi, 0, h)),
            pl.BlockSpec((1, s, LANES), lambda bi, h: (bi, 0, h)),
            const(bias.shape),
        ],
        out_specs=pl.BlockSpec((1, s, LANES), lambda bi, h: (bi, 0, h)),
        out_shape=jax.ShapeDtypeStruct((b, s, MLA_WIDTH), BF16),
        scratch_shapes=[
            pltpu.VMEM((s, 2 * LANES), BF16),
            pltpu.VMEM((nq + 1, t, t), F32),
            pltpu.VMEM((2, t, LANES), F32),
            pltpu.VMEM((2, t, 2 * LANES), F32),
        ],
        compiler_params=pltpu.CompilerParams(
            dimension_semantics=("arbitrary", "arbitrary"), vmem_limit_bytes=VMEM_LIMIT_BYTES),
        name="mla_attn",
    )(qm, km, vm, bias)


def _outffn_kernel(x_ref, osb_ref, omla_ref, mod_ref, gsb_ref, gmla_ref, gffn_ref,
                   wo_ref, wg_ref, wu_ref, wd_ref, out_ref, act_ref):
    gate1 = mod_ref[0, 2:3, :]
    shift2 = mod_ref[0, 3:4, :]
    scale2 = mod_ref[0, 4:5, :]
    gate2 = mod_ref[0, 5:6, :]

    osb = osb_ref[0].astype(F32)
    omla = omla_ref[0].astype(F32)
    nsb = osb * lax.rsqrt(_mean_sq(osb, SB_WIDTH) + EPS) * gsb_ref[...]
    nmla = omla * lax.rsqrt(_mean_sq(omla, MLA_WIDTH) + EPS) * gmla_ref[...]
    mixed = jnp.concatenate([nsb.astype(BF16), nmla.astype(BF16)], axis=1)
    x1 = x_ref[0] + gate1 * _dot(mixed, wo_ref[...])

    h2 = (x1 * lax.rsqrt(_mean_sq(x1, D_MODEL) + EPS)) * (gffn_ref[...] * (1.0 + scale2)) + shift2
    h2 = h2.astype(BF16)
    d_ff = wg_ref.shape[1]
    for cidx in range(d_ff // FFN_TN):
        sl = slice(cidx * FFN_TN, (cidx + 1) * FFN_TN)
        g = _dot(h2, wg_ref[:, sl])
        u = _dot(h2, wu_ref[:, sl])
        act_ref[:, sl] = (g / (1.0 + jnp.exp(-g)) * u).astype(BF16)
    out_ref[0] = x1 + gate2 * _dot(act_ref[...], wd_ref[...])


def _outffn(x, osb, omla, mod3, gsb, gmla, gffn, wo, wg, wu, wd):
    b, s, d = x.shape
    tm = FFN_TM
    d_ff = wg.shape[1]
    const = lambda shape: pl.BlockSpec(shape, lambda i, j: (0,) * len(shape), pipeline_mode=pl.Buffered(1))
    tok = lambda w: pl.BlockSpec((1, tm, w), lambda i, j: (i, j, 0))
    return pl.pallas_call(
        _outffn_kernel,
        grid=(b, s // tm),
        in_specs=[
            tok(d), tok(SB_WIDTH), tok(MLA_WIDTH),
            pl.BlockSpec((1, N_MOD, d), lambda i, j: (i, 0, 0)),
            const((1, SB_WIDTH)), const((1, MLA_WIDTH)), const((1, d)),
            const(wo.shape), const(wg.shape), const(wu.shape), const(wd.shape),
        ],
        out_specs=tok(d),
        out_shape=jax.ShapeDtypeStruct((b, s, d), F32),
        scratch_shapes=[pltpu.VMEM((tm, d_ff), BF16)],
        compiler_params=pltpu.CompilerParams(
            dimension_semantics=("arbitrary", "arbitrary"), vmem_limit_bytes=VMEM_LIMIT_BYTES),
        name="outffn",
    )(x, osb, omla, mod3, gsb, gmla, gffn, wo, wg, wu, wd)


def _swap_halves(n):
    half = n // 2
    return np.concatenate([np.arange(half, n), np.arange(0, half)])


def _layer_params(l, w_in, w_q_up, w_kv_up, q_norm, k_rope_norm):
    swap = _swap_halves(MLA_ROPE_DIM)
    kr0 = _C_CKV + MLA_KV_RANK
    kr = kr0 + np.arange(MLA_ROPE_DIM)
    krs = kr0 + swap
    w1_idx = np.concatenate([np.arange(kr0), kr, kr, krs, krs])
    w1 = w_in[l][:, w1_idx].astype(BF16)

    q_idx = []
    for hd in range(MLA_HEADS):
        base = hd * MLA_QK_DIM
        q_idx += [base + np.arange(MLA_NOPE_DIM), base + MLA_NOPE_DIM + np.arange(MLA_ROPE_DIM),
                  base + MLA_NOPE_DIM + swap]
    wq = w_q_up[l][:, np.concatenate(q_idx)].astype(BF16)

    per_head = MLA_NOPE_DIM + MLA_V_DIM
    kv_idx = ([hd * per_head + np.arange(MLA_NOPE_DIM) for hd in range(MLA_HEADS)]
              + [hd * per_head + MLA_NOPE_DIM + np.arange(MLA_V_DIM) for hd in range(MLA_HEADS)])
    wkv = w_kv_up[l][:, np.concatenate(kv_idx)].astype(BF16)

    gqn = q_norm[l][:MLA_NOPE_DIM].reshape(1, LANES)
    gr = q_norm[l][MLA_NOPE_DIM:]
    gqr = jnp.concatenate([gr, gr[swap]]).reshape(1, LANES)
    gk = k_rope_norm[l]
    gk1 = jnp.concatenate([gk, gk]).reshape(1, LANES)
    gk2 = jnp.concatenate([gk[swap], gk[swap]]).reshape(1, LANES)
    return w1, wq, wkv, gqn, gqr, gk1, gk2


def _constants():
    half = MLA_ROPE_DIM // 2
    freqs = 1.0 / (ROPE_THETA ** (jnp.arange(half, dtype=F32) / half))
    freq = jnp.tile(freqs, LANES // half).reshape(1, LANES)
    sign = np.tile(np.concatenate([-np.ones(half), np.ones(half)]), LANES // (2 * half))
    sign = jnp.asarray(sign, F32).reshape(1, LANES)
    t = ATT_T
    row = np.arange(t)[:, None]
    col = np.arange(t)[None, :]
    tri = jnp.asarray((row >= col).astype(np.float32), BF16)
    bias_strict = jnp.asarray(np.where(col < row, 0.0, MASK_BIAS), F32)
    bias_causal = jnp.asarray(np.where(col <= row, 0.0, MASK_BIAS), F32)
    return freq, sign, tri, bias_strict, bias_causal


def kernel(x, c, positions, w_ada, b_ada, norm_attn, norm_ffn, w_in, q_a_norm, w_q_up, kv_a_norm, w_kv_up,
           q_norm, k_nope_norm, k_rope_norm, out_norm_sb, out_norm_mla, w_out, w_gate, w_up, w_down):
    b, s, d = x.shape
    depth = w_ada.shape[0]
    freq, sign, tri, bias_strict, bias_causal = _constants()
    pos3 = positions.reshape(b, s, 1)
    for l in range(depth):
        mod3 = _adaln(c, w_ada[l], b_ada[l]).reshape(b, N_MOD, d)
        w1, wq, wkv, gqn, gqr, gk1, gk2 = _layer_params(l, w_in, w_q_up, w_kv_up, q_norm, k_rope_norm)
        qkv, qm, km, vm = _inproj(
            x, pos3, mod3, norm_attn[l].reshape(1, d), w1, q_a_norm[l].reshape(1, -1), wq,
            kv_a_norm[l].reshape(1, -1), wkv, gqn, gqr, k_nope_norm[l].reshape(1, LANES), gk1, gk2, freq, sign)
        osb = _sb_attention(qkv, tri, bias_strict)
        omla = _mla_attention(qm, km, vm, bias_causal)
        x = _outffn(x, osb, omla, mod3, out_norm_sb[l].reshape(1, -1), out_norm_mla[l].reshape(1, -1),
                    norm_ffn[l].reshape(1, d), w_out[l].astype(BF16), w_gate[l].astype(BF16),
                    w_up[l].astype(BF16), w_down[l].astype(BF16))
    return x
```

```python
import numpy as np
import jax
import jax.numpy as jnp
from jax import lax
from jax.experimental import pallas as pl
from jax.experimental.pallas import tpu as pltpu

F32 = jnp.float32
BF16 = jnp.bfloat16

D_MODEL = 1024
SB_HEADS = 8
SB_HEAD_DIM = 64
SB_WIDTH = SB_HEADS * SB_HEAD_DIM
MLA_HEADS = 4
MLA_NOPE_DIM = 128
MLA_ROPE_DIM = 64
MLA_QK_DIM = MLA_NOPE_DIM + MLA_ROPE_DIM
MLA_V_DIM = 128
MLA_Q_RANK = 384
MLA_KV_RANK = 256
MLA_WIDTH = MLA_HEADS * MLA_V_DIM
N_MOD = 6
ROPE_THETA = 10000.0
EPS = 1e-6
LOG2E = float(np.log2(np.e))

LANES = 128
VMEM_LIMIT_BYTES = 56 * 1024 * 1024

ADALN_TN = 1024
PROJ_TS = 512
ATT_T = 256
SB_UNROLL = 8
MLA_UNROLL = 8
FFN_TM = 512
FFN_TN = 256
MASK_BIAS = -1e9
SOFTPLUS_LINEAR_ABOVE = 30.0

_C_QSB = 0
_C_KSB = SB_WIDTH
_C_VSB = 2 * SB_WIDTH
_C_CQ = 3 * SB_WIDTH
_C_CKV = _C_CQ + MLA_Q_RANK
_C_KR = _C_CKV + MLA_KV_RANK
_C_KRS = _C_KR + LANES
_N_PROJ = _C_KRS + LANES
_Q_HEAD_W = 2 * LANES


def _nt_dot(a, b):
    return lax.dot_general(a, b, (((1,), (1,)), ((), ())), preferred_element_type=F32)


def _dot(a, b):
    return jnp.dot(a, b, preferred_element_type=F32)


def _mean_sq(x, n):
    return jnp.sum(x * x, axis=-1, keepdims=True) * (1.0 / n)


def _adaln_kernel(c_ref, w_ref, b_ref, o_ref):
    c = c_ref[...]
    sc = c / (1.0 + jnp.exp(-c))
    w = w_ref[...]
    s_hi = sc.astype(BF16)
    s_lo = (sc - s_hi.astype(F32)).astype(BF16)
    w_hi = w.astype(BF16)
    w_lo = (w - w_hi.astype(F32)).astype(BF16)
    acc = _dot(s_hi, w_hi) + _dot(s_lo, w_hi) + _dot(s_hi, w_lo)
    o_ref[...] = acc + b_ref[...]


def _adaln(c, w_ada, b_ada):
    b, d = c.shape
    n = w_ada.shape[1]
    return pl.pallas_call(
        _adaln_kernel,
        grid=(n // ADALN_TN,),
        in_specs=[
            pl.BlockSpec((b, d), lambda j: (0, 0)),
            pl.BlockSpec((d, ADALN_TN), lambda j: (0, j)),
            pl.BlockSpec((1, ADALN_TN), lambda j: (0, j)),
        ],
        out_specs=pl.BlockSpec((b, ADALN_TN), lambda j: (0, j)),
        out_shape=jax.ShapeDtypeStruct((b, n), F32),
        compiler_params=pltpu.CompilerParams(
            dimension_semantics=("arbitrary",), vmem_limit_bytes=VMEM_LIMIT_BYTES),
        name="adaln",
    )(c, w_ada, b_ada.reshape(1, n))


def _rope_tables(pos, freq):
    half = MLA_ROPE_DIM // 2
    n_grp = LANES // half
    rows = pos.shape[0] // n_grp
    grp = lax.shift_right_logical(lax.broadcasted_iota(jnp.int32, (1, LANES), 1), int(np.log2(half)))
    pos4 = jnp.broadcast_to(pos[0:rows], (rows, LANES))
    for g in range(1, n_grp):
        pos4 = jnp.where(grp == g, jnp.broadcast_to(pos[g * rows:(g + 1) * rows], (rows, LANES)), pos4)
    ang4 = pos4 * freq

    def expand(x4):
        blocks = []
        for g in range(n_grp):
            xg = jnp.where(grp == g, x4, 0.0)
            y = xg
            for k in range(1, n_grp):
                y = y + pltpu.roll(xg, k * half, axis=1)
            blocks.append(y)
        return jnp.concatenate(blocks, axis=0)

    return expand(jnp.cos(ang4)), expand(jnp.sin(ang4))


def _inproj_kernel(x_ref, pos_ref, mod_ref, gattn_ref, w1_ref, gqa_ref, wq_ref, gkva_ref, wkv_ref,
                   gqn_ref, gqr_ref, gkn_ref, gk1_ref, gk2_ref, freq_ref, sign_ref,
                   qkv_ref, qm_ref, km_ref, vm_ref):
    x = x_ref[0]
    shift1 = mod_ref[0, 0:1, :]
    scale1 = mod_ref[0, 1:2, :]
    r = lax.rsqrt(_mean_sq(x, D_MODEL) + EPS)
    h = (x * r) * (gattn_ref[...] * (1.0 + scale1)) + shift1
    proj = _dot(h.astype(BF16), w1_ref[...])

    qkv_ref[0, :, 0:SB_WIDTH] = (proj[:, _C_QSB:_C_KSB] * (SB_HEAD_DIM ** -0.5)).astype(BF16)
    qkv_ref[0, :, SB_WIDTH:3 * SB_WIDTH] = proj[:, _C_KSB:_C_CQ].astype(BF16)

    cos_t, sin_t = _rope_tables(pos_ref[0].astype(F32), freq_ref[...])
    sin_t = sin_t * sign_ref[...]

    cq = proj[:, _C_CQ:_C_CKV]
    cqn = cq * lax.rsqrt(_mean_sq(cq, MLA_Q_RANK) + EPS) * gqa_ref[...]
    y = _dot(cqn.astype(BF16), wq_ref[...])
    lane = lax.broadcasted_iota(jnp.int32, (1, LANES), 1)
    rope_q = jnp.where(lane < MLA_ROPE_DIM, cos_t, sin_t) * gqr_ref[...]
    for hd in range(MLA_HEADS):
        yn = y[:, hd * _Q_HEAD_W: hd * _Q_HEAD_W + LANES]
        yr = y[:, hd * _Q_HEAD_W + LANES: (hd + 1) * _Q_HEAD_W]
        ssq = jnp.sum(yn * yn, axis=-1, keepdims=True) + 0.5 * jnp.sum(yr * yr, axis=-1, keepdims=True)
        rq = lax.rsqrt(ssq * (1.0 / MLA_QK_DIM) + EPS) * (MLA_QK_DIM ** -0.5 * LOG2E)
        qm_ref[0, :, hd * _Q_HEAD_W: hd * _Q_HEAD_W + LANES] = (yn * rq * gqn_ref[...]).astype(BF16)
        qm_ref[0, :, hd * _Q_HEAD_W + LANES: (hd + 1) * _Q_HEAD_W] = (yr * rq * rope_q).astype(BF16)

    z1 = proj[:, _C_KR:_C_KRS]
    z2 = proj[:, _C_KRS:_N_PROJ]
    rk = lax.rsqrt(_mean_sq(z1, LANES) + EPS)
    k_rope = (rk * (z1 * (gk1_ref[...] * cos_t) + z2 * (gk2_ref[...] * sin_t))).astype(BF16)

    ckv = proj[:, _C_CKV:_C_KR]
    ckvn = ckv * lax.rsqrt(_mean_sq(ckv, MLA_KV_RANK) + EPS) * gkva_ref[...]
    kv = _dot(ckvn.astype(BF16), wkv_ref[...])
    for hd in range(MLA_HEADS):
        kn = kv[:, hd * LANES:(hd + 1) * LANES]
        km_ref[0, :, hd * _Q_HEAD_W: hd * _Q_HEAD_W + LANES] = (
            kn * lax.rsqrt(_mean_sq(kn, MLA_NOPE_DIM) + EPS) * gkn_ref[...]).astype(BF16)
        km_ref[0, :, hd * _Q_HEAD_W + LANES: (hd + 1) * _Q_HEAD_W] = k_rope
    vm_ref[0] = kv[:, MLA_WIDTH:].astype(BF16)


def _inproj(x, pos3, mod3, gattn, w1, gqa, wq, gkva, wkv, gqn, gqr, gkn, gk1, gk2, freq, sign):
    b, s, d = x.shape
    ts = PROJ_TS
    const = lambda shape: pl.BlockSpec(shape, lambda i, j: (0,) * len(shape), pipeline_mode=pl.Buffered(1))
    tok = lambda w: pl.BlockSpec((1, ts, w), lambda i, j: (i, j, 0))
    return pl.pallas_call(
        _inproj_kernel,
        grid=(b, s // ts),
        in_specs=[
            tok(d),
            tok(1),
            pl.BlockSpec((1, N_MOD, d), lambda i, j: (i, 0, 0)),
            const((1, d)),
            const(w1.shape),
            const((1, MLA_Q_RANK)),
            const(wq.shape),
            const((1, MLA_KV_RANK)),
            const(wkv.shape),
            const((1, LANES)), const((1, LANES)), const((1, LANES)), const((1, LANES)), const((1, LANES)),
            const((1, LANES)), const((1, LANES)),
        ],
        out_specs=[tok(3 * SB_WIDTH), tok(MLA_HEADS * _Q_HEAD_W), tok(MLA_HEADS * _Q_HEAD_W), tok(MLA_WIDTH)],
        out_shape=[
            jax.ShapeDtypeStruct((b, s, 3 * SB_WIDTH), BF16),
            jax.ShapeDtypeStruct((b, s, MLA_HEADS * _Q_HEAD_W), BF16),
            jax.ShapeDtypeStruct((b, s, MLA_HEADS * _Q_HEAD_W), BF16),
            jax.ShapeDtypeStruct((b, s, MLA_WIDTH), BF16),
        ],
        compiler_params=pltpu.CompilerParams(
            dimension_semantics=("arbitrary", "arbitrary"), vmem_limit_bytes=VMEM_LIMIT_BYTES),
        name="inproj",
    )(x, pos3, mod3, gattn, w1, gqa, wq, gkva, wkv, gqn, gqr, gkn, gk1, gk2, freq, sign)


def _groups(c, nq):
    return ((0, c, 0), (1, nq - 1 - c, c + 1))


def _static_loop(n, body, unroll):
    if n <= unroll:
        for j in range(n):
            body(j)
    else:
        def step(j, carry):
            body(j)
            return carry
        lax.fori_loop(0, n, step, 0, unroll=unroll)


def _rows(j, t):
    if isinstance(j, int):
        return pl.ds(j * t, t)
    return pl.ds(pl.multiple_of(j * t, t), t)


def _sb_kernel(q_ref, k_ref, v_ref, tri_ref, bias_ref, o_ref):
    t = ATT_T
    nq = k_ref.shape[1] // t
    lane = lax.broadcasted_iota(jnp.int32, (1, LANES), 1)
    first = lane < SB_HEAD_DIM

    def q_block(qb):
        q = q_ref[0, _rows(qb, t), :]
        qh = (jnp.where(first, q, jnp.zeros_like(q)), jnp.where(first, jnp.zeros_like(q), q))
        z, sp = {}, {}
        for j in range(qb + 1):
            k = k_ref[0, _rows(j, t), :]
            for hh in range(2):
                zz = _nt_dot(qh[hh], k)
                if j == qb:
                    zz = zz + bias_ref[...]
                z[j, hh] = zz
                soft = jnp.log(1.0 + jnp.exp2(zz * LOG2E))
                sp[j, hh] = jnp.where(zz > SOFTPLUS_LINEAR_ABOVE, zz, soft).astype(BF16)
        outs = []
        for hh in range(2):
            run = jnp.zeros((t, LANES), F32)
            acc = jnp.zeros((t, LANES), F32)
            for j in range(qb, -1, -1):
                cin = _dot(sp[j, hh], tri_ref[...])
                arg = z[j, hh] - cin - jnp.concatenate([run] * (t // LANES), axis=1)
                w = jnp.exp2(arg * LOG2E).astype(BF16)
                acc = acc + _dot(w, v_ref[0, _rows(j, t), :])
                run = run + jnp.broadcast_to(cin[:, 0:1], (t, LANES))
            outs.append(acc)
        o_ref[0, _rows(qb, t), :] = jnp.where(first, outs[0], outs[1]).astype(BF16)

    for c in range(nq // 2):
        q_block(c)
        q_block(nq - 1 - c)


def _sb_attention(qkv, tri, bias):
    b, s, _ = qkv.shape
    t = ATT_T
    nq = s // t
    n_pairs = SB_WIDTH // LANES
    const = lambda shape: pl.BlockSpec(shape, lambda bi, p: (0,) * len(shape), pipeline_mode=pl.Buffered(1))
    return pl.pallas_call(
        _sb_kernel,
        grid=(b, n_pairs),
        in_specs=[
            pl.BlockSpec((1, s, LANES), lambda bi, p: (bi, 0, p)),
            pl.BlockSpec((1, s, LANES), lambda bi, p: (bi, 0, n_pairs + p)),
            pl.BlockSpec((1, s, LANES), lambda bi, p: (bi, 0, 2 * n_pairs + p)),
            const(tri.shape),
            const(bias.shape),
        ],
        out_specs=pl.BlockSpec((1, s, LANES), lambda bi, p: (bi, 0, p)),
        out_shape=jax.ShapeDtypeStruct((b, s, SB_WIDTH), BF16),
        compiler_params=pltpu.CompilerParams(
            dimension_semantics=("arbitrary", "arbitrary"), vmem_limit_bytes=VMEM_LIMIT_BYTES),
        name="sb_attn",
    )(qkv, qkv, qkv, tri, bias)


def _mla_kernel(q_ref, k_ref, v_ref, bias_ref, o_ref, vaug_ref, s_ref, mx_ref, acc_ref):
    t = ATT_T
    nq = k_ref.shape[1] // t

    vaug_ref[:, 0:LANES] = v_ref[0]
    vaug_ref[:, LANES:2 * LANES] = jnp.ones((v_ref.shape[1], LANES), BF16)

    def scores(sel, qb, j, slot, diag):
        s = _nt_dot(q_ref[0, _rows(qb, t), :], k_ref[0, _rows(j, t), :])
        if diag:
            s = s + bias_ref[...]
        s_ref[slot] = s
        colmax = s[:, 0:LANES]
        for cb in range(1, t // LANES):
            colmax = jnp.maximum(colmax, s[:, cb * LANES:(cb + 1) * LANES])
        mx_ref[sel] = jnp.maximum(mx_ref[sel], colmax)

    def accumulate(sel, j, slot):
        m = mx_ref[sel]
        p = jnp.exp2(s_ref[slot] - jnp.concatenate([m] * (t // LANES), axis=1)).astype(BF16)
        acc_ref[sel] += _dot(p, vaug_ref[_rows(j, t), :])

    for c in range(nq // 2):
        q_blocks = (c, nq - 1 - c)
        mx_ref[...] = jnp.full(mx_ref.shape, MASK_BIAS, F32)
        acc_ref[...] = jnp.zeros(acc_ref.shape, F32)
        for sel, n_off, base in _groups(c, nq):
            qb = q_blocks[sel]
            _static_loop(n_off, lambda j, sel=sel, qb=qb, base=base: scores(sel, qb, j, base + j, False),
                         MLA_UNROLL)
            scores(sel, qb, n_off, base + n_off, True)
            mx_ref[sel] = jnp.broadcast_to(jnp.max(mx_ref[sel], axis=-1, keepdims=True), (t, LANES))
        for sel, n_off, base in _groups(c, nq):
            _static_loop(n_off + 1, lambda j, sel=sel, base=base: accumulate(sel, j, base + j), MLA_UNROLL)
        for sel, qb in enumerate(q_blocks):
            acc = acc_ref[sel]
            o_ref[0, _rows(qb, t), :] = (acc[:, 0:LANES] / acc[:, LANES:2 * LANES]).astype(BF16)


def _mla_attention(qm, km, vm, bias):
    b, s, _ = qm.shape
    t = ATT_T
    nq = s // t
    const = lambda shape: pl.BlockSpec(shape, lambda bi, h: (0,) * len(shape), pipeline_mode=pl.Buffered(1))
    return pl.pallas_call(
        _mla_kernel,
        grid=(b, MLA_HEADS),
        in_specs=[
            pl.BlockSpec((1, s, _Q_HEAD_W), lambda bi, h: (bi, 0, h)),
            pl.BlockSpec((1, s, _Q_HEAD_W), lambda bi, h: (bi, 0, h)),
            pl.BlockSpec((1, s, LANES), lambda bi, h: (bi, 0, h)),
            const(bias.shape),
        ],
        out_specs=pl.BlockSpec((1, s, LANES), lambda bi, h: (bi, 0, h)),
        out_shape=jax.ShapeDtypeStruct((b, s, MLA_WIDTH), BF16),
        scratch_shapes=[
            pltpu.VMEM((s, 2 * LANES), BF16),
            pltpu.VMEM((nq + 1, t, t), F32),
            pltpu.VMEM((2, t, LANES), F32),
            pltpu.VMEM((2, t, 2 * LANES), F32),
        ],
        compiler_params=pltpu.CompilerParams(
            dimension_semantics=("arbitrary", "arbitrary"), vmem_limit_bytes=VMEM_LIMIT_BYTES),
        name="mla_attn",
    )(qm, km, vm, bias)


def _outffn_kernel(x_ref, osb_ref, omla_ref, mod_ref, gsb_ref, gmla_ref, gffn_ref,
                   wo_ref, wg_ref, wu_ref, wd_ref, out_ref, act_ref):
    gate1 = mod_ref[0, 2:3, :]
    shift2 = mod_ref[0, 3:4, :]
    scale2 = mod_ref[0, 4:5, :]
    gate2 = mod_ref[0, 5:6, :]

    osb = osb_ref[0].astype(F32)
    omla = omla_ref[0].astype(F32)
    nsb = osb * lax.rsqrt(_mean_sq(osb, SB_WIDTH) + EPS) * gsb_ref[...]
    nmla = omla * lax.rsqrt(_mean_sq(omla, MLA_WIDTH) + EPS) * gmla_ref[...]
    mixed = jnp.concatenate([nsb.astype(BF16), nmla.astype(BF16)], axis=1)
    x1 = x_ref[0] + gate1 * _dot(mixed, wo_ref[...])

    h2 = (x1 * lax.rsqrt(_mean_sq(x1, D_MODEL) + EPS)) * (gffn_ref[...] * (1.0 + scale2)) + shift2
    h2 = h2.astype(BF16)
    d_ff = wg_ref.shape[1]
    for cidx in range(d_ff // FFN_TN):
        sl = slice(cidx * FFN_TN, (cidx + 1) * FFN_TN)
        g = _dot(h2, wg_ref[:, sl])
        u = _dot(h2, wu_ref[:, sl])
        act_ref[:, sl] = (g / (1.0 + jnp.exp(-g)) * u).astype(BF16)
    out_ref[0] = x1 + gate2 * _dot(act_ref[...], wd_ref[...])


def _outffn(x, osb, omla, mod3, gsb, gmla, gffn, wo, wg, wu, wd):
    b, s, d = x.shape
    tm = FFN_TM
    d_ff = wg.shape[1]
    const = lambda shape: pl.BlockSpec(shape, lambda i, j: (0,) * len(shape), pipeline_mode=pl.Buffered(1))
    tok = lambda w: pl.BlockSpec((1, tm, w), lambda i, j: (i, j, 0))
    return pl.pallas_call(
        _outffn_kernel,
        grid=(b, s // tm),
        in_specs=[
            tok(d), tok(SB_WIDTH), tok(MLA_WIDTH),
            pl.BlockSpec((1, N_MOD, d), lambda i, j: (i, 0, 0)),
            const((1, SB_WIDTH)), const((1, MLA_WIDTH)), const((1, d)),
            const(wo.shape), const(wg.shape), const(wu.shape), const(wd.shape),
        ],
        out_specs=tok(d),
        out_shape=jax.ShapeDtypeStruct((b, s, d), F32),
        scratch_shapes=[pltpu.VMEM((tm, d_ff), BF16)],
        compiler_params=pltpu.CompilerParams(
            dimension_semantics=("arbitrary", "arbitrary"), vmem_limit_bytes=VMEM_LIMIT_BYTES),
        name="outffn",
    )(x, osb, omla, mod3, gsb, gmla, gffn, wo, wg, wu, wd)


def _swap_halves(a):
    half = a.shape[-1] // 2
    return jnp.concatenate([a[..., half:], a[..., :half]], axis=-1)


def _layer_params(l, w_in, w_q_up, w_kv_up, q_norm, k_rope_norm):
    w = w_in[l]
    kr = w[:, _C_CKV + MLA_KV_RANK:]
    w1 = jnp.concatenate([w, kr, _swap_halves(kr), _swap_halves(kr)], axis=1).astype(BF16)

    d_q = w_q_up.shape[1]
    wq4 = w_q_up[l].reshape(d_q, MLA_HEADS, MLA_QK_DIM)
    wq = jnp.concatenate([wq4, _swap_halves(wq4[:, :, MLA_NOPE_DIM:])], axis=2)
    wq = wq.reshape(d_q, MLA_HEADS * _Q_HEAD_W).astype(BF16)

    d_kv = w_kv_up.shape[1]
    wkv = w_kv_up[l].reshape(d_kv, MLA_HEADS, 2, MLA_NOPE_DIM).transpose(0, 2, 1, 3)
    wkv = wkv.reshape(d_kv, 2 * MLA_HEADS * MLA_NOPE_DIM).astype(BF16)

    gqn = q_norm[l][:MLA_NOPE_DIM].reshape(1, LANES)
    gr = q_norm[l][MLA_NOPE_DIM:]
    gqr = jnp.concatenate([gr, _swap_halves(gr)]).reshape(1, LANES)
    gk = k_rope_norm[l]
    gk1 = jnp.concatenate([gk, gk]).reshape(1, LANES)
    gk2 = jnp.concatenate([_swap_halves(gk), _swap_halves(gk)]).reshape(1, LANES)
    return w1, wq, wkv, gqn, gqr, gk1, gk2


def _constants():
    half = MLA_ROPE_DIM // 2
    freqs = 1.0 / (ROPE_THETA ** (jnp.arange(half, dtype=F32) / half))
    freq = jnp.tile(freqs, LANES // half).reshape(1, LANES)
    sign = np.tile(np.concatenate([-np.ones(half), np.ones(half)]), LANES // (2 * half))
    sign = jnp.asarray(sign, F32).reshape(1, LANES)
    t = ATT_T
    row = np.arange(t)[:, None]
    col = np.arange(t)[None, :]
    tri = jnp.asarray((row >= col).astype(np.float32), BF16)
    bias_strict = jnp.asarray(np.where(col < row, 0.0, MASK_BIAS), F32)
    bias_causal = jnp.asarray(np.where(col <= row, 0.0, MASK_BIAS), F32)
    return freq, sign, tri, bias_strict, bias_causal


def kernel(x, c, positions, w_ada, b_ada, norm_attn, norm_ffn, w_in, q_a_norm, w_q_up, kv_a_norm, w_kv_up,
           q_norm, k_nope_norm, k_rope_norm, out_norm_sb, out_norm_mla, w_out, w_gate, w_up, w_down):
    b, s, d = x.shape
    depth = w_ada.shape[0]
    freq, sign, tri, bias_strict, bias_causal = _constants()
    pos3 = positions.reshape(b, s, 1)
    for l in range(depth):
        mod3 = _adaln(c, w_ada[l], b_ada[l]).reshape(b, N_MOD, d)
        w1, wq, wkv, gqn, gqr, gk1, gk2 = _layer_params(l, w_in, w_q_up, w_kv_up, q_norm, k_rope_norm)
        qkv, qm, km, vm = _inproj(
            x, pos3, mod3, norm_attn[l].reshape(1, d), w1, q_a_norm[l].reshape(1, -1), wq,
            kv_a_norm[l].reshape(1, -1), wkv, gqn, gqr, k_nope_norm[l].reshape(1, LANES), gk1, gk2, freq, sign)
        osb = _sb_attention(qkv, tri, bias_strict)
        omla = _mla_attention(qm, km, vm, bias_causal)
        x = _outffn(x, osb, omla, mod3, out_norm_sb[l].reshape(1, -1), out_norm_mla[l].reshape(1, -1),
                    norm_ffn[l].reshape(1, d), w_out[l].astype(BF16), w_gate[l].astype(BF16),
                    w_up[l].astype(BF16), w_down[l].astype(BF16))
    return x
```

```python
import numpy as np
import jax
import jax.numpy as jnp
from jax import lax
from jax.experimental import pallas as pl
from jax.experimental.pallas import tpu as pltpu

F32 = jnp.float32
BF16 = jnp.bfloat16

D_MODEL = 1024
SB_HEADS = 8
SB_HEAD_DIM = 64
SB_WIDTH = SB_HEADS * SB_HEAD_DIM
MLA_HEADS = 4
MLA_NOPE_DIM = 128
MLA_ROPE_DIM = 64
MLA_QK_DIM = MLA_NOPE_DIM + MLA_ROPE_DIM
MLA_V_DIM = 128
MLA_Q_RANK = 384
MLA_KV_RANK = 256
MLA_WIDTH = MLA_HEADS * MLA_V_DIM
N_MOD = 6
N_MOD_EARLY = 2
ROPE_THETA = 10000.0
EPS = 1e-6
LOG2E = float(np.log2(np.e))

LANES = 128
BF16_SUBLANES = 16
VMEM_LIMIT_BYTES = 56 * 1024 * 1024

ADALN_TN = 1024
PROJ_TS = 512
ATT_T = 256
FFN_TM = 512
FFN_TN = 256
MASK_BIAS = -1e9
SOFTPLUS_LINEAR_ABOVE = 30.0

_C_QSB = 0
_C_KSB = SB_WIDTH
_C_VSB = 2 * SB_WIDTH
_C_CQ = 3 * SB_WIDTH
_C_CKV = _C_CQ + MLA_Q_RANK
_C_KR = _C_CKV + MLA_KV_RANK
_N_IN = _C_KR + MLA_ROPE_DIM
_N_PROJ = _C_KR + LANES
_Q_HEAD_W = 2 * LANES
assert MLA_NOPE_DIM == LANES and MLA_V_DIM == LANES and 2 * MLA_ROPE_DIM == LANES


def _nt_dot(a, b):
    return lax.dot_general(a, b, (((1,), (1,)), ((), ())), preferred_element_type=F32)


def _dot(a, b):
    return jnp.dot(a, b, preferred_element_type=F32)


def _mean_sq(x, n):
    return jnp.sum(x * x, axis=-1, keepdims=True) * (1.0 / n)


def _adaln_columns(c, w, bias):
    sc = c / (1.0 + jnp.exp(-c))
    s_hi = sc.astype(BF16)
    s_lo = (sc - s_hi.astype(F32)).astype(BF16)
    w_hi = w.astype(BF16)
    w_lo = (w - w_hi.astype(F32)).astype(BF16)
    return _dot(s_hi, w_hi) + _dot(s_lo, w_hi) + _dot(s_hi, w_lo) + bias


def _adaln_kernel(c_ref, w_ref, b_ref, o_ref):
    o_ref[...] = _adaln_columns(c_ref[...], w_ref[...], b_ref[...])


def _adaln(c, w_ada, b_ada2, n_cols):
    b, d = c.shape
    return pl.pallas_call(
        _adaln_kernel,
        grid=(n_cols // ADALN_TN,),
        in_specs=[
            pl.BlockSpec((b, d), lambda j: (0, 0)),
            pl.BlockSpec((d, ADALN_TN), lambda j: (0, j)),
            pl.BlockSpec((1, ADALN_TN), lambda j: (0, j)),
        ],
        out_specs=pl.BlockSpec((b, ADALN_TN), lambda j: (0, j)),
        out_shape=jax.ShapeDtypeStruct((b, n_cols), F32),
        compiler_params=pltpu.CompilerParams(
            dimension_semantics=("arbitrary",), vmem_limit_bytes=VMEM_LIMIT_BYTES),
        name="adaln",
    )(c, w_ada, b_ada2)


def _rope_tables(pos4, freq):
    half = MLA_ROPE_DIM // 2
    n_grp = LANES // half
    grp = lax.shift_right_logical(lax.broadcasted_iota(jnp.int32, (1, LANES), 1), int(np.log2(half)))
    ang4 = pos4 * freq

    def expand(x4):
        blocks = []
        for g in range(n_grp):
            xg = jnp.where(grp == g, x4, 0.0)
            y = xg
            for k in range(1, n_grp):
                y = y + pltpu.roll(xg, k * half, axis=1)
            blocks.append(y)
        return jnp.concatenate(blocks, axis=0)

    return expand(jnp.cos(ang4)), expand(jnp.sin(ang4))


def _inproj_kernel(x_ref, pos4_ref, mod_ref, gattn_ref, win_ref, gqa_ref, wq_ref, gkva_ref, wkv_ref,
                   gqn_ref, gqr_ref, gkn_ref, gk1_ref, gk2_ref, freq_ref, sign_ref,
                   wo_ref, wg_ref, wu_ref, wd_ref, c_ref, wada_ref, bada_ref,
                   qkv_ref, qm_ref, km_ref, vm_ref, wo_bf_ref, wg_bf_ref, wu_bf_ref, wd_bf_ref, mod_late_ref,
                   w1_ref):
    @pl.when(jnp.logical_and(pl.program_id(0) == 0, pl.program_id(1) == 0))
    def _():
        for c0 in range(0, _C_KR, LANES):
            w1_ref[:, c0:c0 + LANES] = win_ref[:, c0:c0 + LANES].astype(BF16)
        w1_ref[:, _C_KR:_N_IN] = win_ref[:, _C_KR:_N_IN].astype(BF16)
        w1_ref[:, _N_IN:_N_PROJ] = jnp.zeros((w1_ref.shape[0], _N_PROJ - _N_IN), BF16)

    cos_t, sin_t = _rope_tables(pos4_ref[0, 0], freq_ref[...])
    sin_t = sin_t * sign_ref[...]

    x = x_ref[0]
    shift1 = mod_ref[0, 0:1, :]
    scale1 = mod_ref[0, 1:2, :]
    r = lax.rsqrt(_mean_sq(x, D_MODEL) + EPS)
    h = ((x * r) * (gattn_ref[...] * (1.0 + scale1)) + shift1).astype(BF16)
    proj = _dot(h, w1_ref[:, _C_CQ:_N_PROJ])
    o_cq, o_ckv, o_kr = 0, _C_CKV - _C_CQ, _C_KR - _C_CQ

    cq = proj[:, o_cq:o_ckv]
    cqn = cq * lax.rsqrt(_mean_sq(cq, MLA_Q_RANK) + EPS) * gqa_ref[...]
    y = _dot(cqn.astype(BF16), wq_ref[...])
    lane = lax.broadcasted_iota(jnp.int32, (1, LANES), 1)
    rope_q = jnp.where(lane < MLA_ROPE_DIM, cos_t, sin_t) * gqr_ref[...]
    for hd in range(MLA_HEADS):
        yn = y[:, hd * _Q_HEAD_W: hd * _Q_HEAD_W + LANES]
        yr = y[:, hd * _Q_HEAD_W + LANES: (hd + 1) * _Q_HEAD_W]
        ssq = jnp.sum(yn * yn, axis=-1, keepdims=True) + 0.5 * jnp.sum(yr * yr, axis=-1, keepdims=True)
        rq = lax.rsqrt(ssq * (1.0 / MLA_QK_DIM) + EPS) * (MLA_QK_DIM ** -0.5 * LOG2E)
        qm_ref[0, :, hd * _Q_HEAD_W: hd * _Q_HEAD_W + LANES] = (yn * rq * gqn_ref[...]).astype(BF16)
        qm_ref[0, :, hd * _Q_HEAD_W + LANES: (hd + 1) * _Q_HEAD_W] = (yr * rq * rope_q).astype(BF16)

    kr0 = proj[:, o_kr:o_kr + LANES]
    z1 = kr0 + pltpu.roll(kr0, MLA_ROPE_DIM, axis=1)
    z2 = pltpu.roll(z1, MLA_ROPE_DIM // 2, axis=1)
    rk = lax.rsqrt(_mean_sq(z1, LANES) + EPS)
    k_rope = (rk * (z1 * (gk1_ref[...] * cos_t) + z2 * (gk2_ref[...] * sin_t))).astype(BF16)

    ckv = proj[:, o_ckv:o_kr]
    ckvn = ckv * lax.rsqrt(_mean_sq(ckv, MLA_KV_RANK) + EPS) * gkva_ref[...]
    kv = _dot(ckvn.astype(BF16), wkv_ref[...])
    for hd in range(MLA_HEADS):
        kn = kv[:, hd * _Q_HEAD_W: hd * _Q_HEAD_W + LANES]
        km_ref[0, :, hd * _Q_HEAD_W: hd * _Q_HEAD_W + LANES] = (
            kn * lax.rsqrt(_mean_sq(kn, MLA_NOPE_DIM) + EPS) * gkn_ref[...]).astype(BF16)
        km_ref[0, :, hd * _Q_HEAD_W + LANES: (hd + 1) * _Q_HEAD_W] = k_rope
        vm_ref[0, :, hd * LANES:(hd + 1) * LANES] = kv[:, hd * _Q_HEAD_W + LANES: (hd + 1) * _Q_HEAD_W].astype(BF16)

    proj_sb = _dot(h, w1_ref[:, _C_QSB:_C_CQ])
    qkv_ref[0, :, 0:SB_WIDTH] = (proj_sb[:, _C_QSB:_C_KSB] * (SB_HEAD_DIM ** -0.5)).astype(BF16)
    qkv_ref[0, :, SB_WIDTH:3 * SB_WIDTH] = proj_sb[:, _C_KSB:_C_CQ].astype(BF16)

    for src, dst in ((wo_ref, wo_bf_ref), (wg_ref, wg_bf_ref), (wu_ref, wu_bf_ref), (wd_ref, wd_bf_ref)):
        dst[...] = src[...].astype(BF16)
    mod_late_ref[...] = _adaln_columns(c_ref[...], wada_ref[...], bada_ref[...])


def _inproj(x, pos4, mod3, gattn, w_in, gqa, wq, gkva, wkv, gqn, gqr, gkn, gk1, gk2, freq, sign, later_weights,
            c, w_ada, b_ada2):
    b, s, d = x.shape
    ts = PROJ_TS
    n_tok = s // ts
    n_early = mod3.shape[1] * d
    late_cols = (w_ada.shape[1] - n_early) // (b * n_tok)
    assert late_cols % LANES == 0 and late_cols * b * n_tok == w_ada.shape[1] - n_early
    late = lambda rows: pl.BlockSpec((rows, late_cols), lambda i, j: (0, n_early // late_cols + i * n_tok + j))
    const = lambda shape: pl.BlockSpec(shape, lambda i, j: (0,) * len(shape), pipeline_mode=pl.Buffered(1))
    tok = lambda w: pl.BlockSpec((1, ts, w), lambda i, j: (i, j, 0))

    def cast_slice(w):
        rows = w.shape[0]
        n_steps = b * n_tok
        while rows % (n_steps * BF16_SUBLANES):
            n_steps //= 2
        per = (b * n_tok) // n_steps
        return pl.BlockSpec((rows // n_steps, w.shape[1]), lambda i, j: ((i * n_tok + j) // per, 0))

    cast_specs = [cast_slice(w) for w in later_weights]
    return pl.pallas_call(
        _inproj_kernel,
        grid=(b, s // ts),
        in_specs=[
            tok(d),
            pl.BlockSpec((1, 1, ts // (LANES // (MLA_ROPE_DIM // 2)), LANES), lambda i, j: (i, j, 0, 0)),
            pl.BlockSpec((1, mod3.shape[1], d), lambda i, j: (i, 0, 0)),
            const((1, d)),
            const(w_in.shape),
            const((1, MLA_Q_RANK)),
            const(wq.shape),
            const((1, MLA_KV_RANK)),
            const(wkv.shape),
            const((1, LANES)), const((1, LANES)), const((1, LANES)), const((1, LANES)), const((1, LANES)),
            const((1, LANES)), const((1, LANES)),
        ] + cast_specs + [const(c.shape), late(d), late(1)],
        out_specs=[tok(3 * SB_WIDTH), tok(MLA_HEADS * _Q_HEAD_W), tok(MLA_HEADS * _Q_HEAD_W), tok(MLA_WIDTH)]
        + cast_specs + [pl.BlockSpec((b, late_cols), lambda i, j: (0, i * n_tok + j))],
        out_shape=[
            jax.ShapeDtypeStruct((b, s, 3 * SB_WIDTH), BF16),
            jax.ShapeDtypeStruct((b, s, MLA_HEADS * _Q_HEAD_W), BF16),
            jax.ShapeDtypeStruct((b, s, MLA_HEADS * _Q_HEAD_W), BF16),
            jax.ShapeDtypeStruct((b, s, MLA_WIDTH), BF16),
        ] + [jax.ShapeDtypeStruct(w.shape, BF16) for w in later_weights]
        + [jax.ShapeDtypeStruct((b, w_ada.shape[1] - n_early), F32)],
        scratch_shapes=[pltpu.VMEM((d, _N_PROJ), BF16)],
        compiler_params=pltpu.CompilerParams(
            dimension_semantics=("arbitrary", "arbitrary"), vmem_limit_bytes=VMEM_LIMIT_BYTES),
        name="inproj",
    )(x, pos4, mod3, gattn, w_in, gqa, wq, gkva, wkv, gqn, gqr, gkn, gk1, gk2, freq, sign, *later_weights,
      c, w_ada, b_ada2)


def _rows(j, t):
    return pl.ds(j * t, t)


def _sb_tasks(qb, q_ref, k_ref, v_ref, tri_ref, bias_ref, o_ref):
    t = ATT_T
    first = lax.broadcasted_iota(jnp.int32, (1, LANES), 1) < SB_HEAD_DIM
    q = q_ref[0, _rows(qb, t), :]
    qh = (jnp.where(first, q, jnp.zeros_like(q)), jnp.where(first, jnp.zeros_like(q), q))

    def softplus_bf16(zz):
        soft = jnp.log(1.0 + jnp.exp2(zz * LOG2E))
        return jnp.where(zz > SOFTPLUS_LINEAR_ABOVE, zz, soft).astype(BF16)

    half = t // 2

    def quadrants(full):
        return full[0:half, 0:half], full[half:t, :]

    def assemble(top_left, bottom):
        top = jnp.concatenate([top_left, jnp.zeros((half, t - half), top_left.dtype)], axis=1)
        return jnp.concatenate([top, bottom], axis=0)

    heads = (0, 1)
    q_both = jnp.concatenate(qh, axis=0)
    z, sp = {}, {}

    def score(j):
        zz_both = _nt_dot(q_both, k_ref[0, _rows(j, t), :])
        parts = []
        for hh in heads:
            zz = zz_both[hh * t:(hh + 1) * t]
            if j == qb:
                z[j, hh] = tuple(a + b for a, b in zip(quadrants(zz), quadrants(bias_ref[...])))
                parts.append(assemble(*(softplus_bf16(part) for part in z[j, hh])))
            else:
                z[j, hh] = zz
                parts.append(softplus_bf16(zz))
        sp[j] = jnp.concatenate(parts, axis=0)

    run, acc = {}, {}

    def weigh(j):
        cin_both = _dot(sp[j], tri_ref[...])
        ws = []
        for hh in heads:
            cin = cin_both[hh * t:(hh + 1) * t]
            if j == qb:
                ws.append(assemble(*(jnp.exp2((zp - cp) * LOG2E).astype(BF16)
                                     for zp, cp in zip(z[j, hh], quadrants(cin)))))
            else:
                arg = z[j, hh] - cin - jnp.concatenate([run[hh]] * (t // LANES), axis=1)
                ws.append(jnp.exp2(arg * LOG2E).astype(BF16))
            tot = jnp.broadcast_to(cin[:, 0:1], (t, LANES))
            run[hh] = tot if j == qb else run[hh] + tot
        pv_both = _dot(jnp.concatenate(ws, axis=0), v_ref[0, _rows(j, t), :])
        for hh in heads:
            pv = pv_both[hh * t:(hh + 1) * t]
            acc[hh] = pv if j == qb else acc[hh] + pv
        if j == 0:
            o_ref[0, _rows(qb, t), :] = jnp.where(first, acc[0], acc[1]).astype(BF16)

    score_tasks = [lambda j=j: score(j) for j in range(qb + 1)]
    weigh_tasks = [lambda j=j: weigh(j) for j in range(qb, -1, -1)]
    return score_tasks, weigh_tasks


def _mla_tasks(qb, q_ref, k_ref, vaug_ref, bias_ref, o_ref):
    t = ATT_T
    n_col = t // LANES
    st = {"s": {}, "colmax": None, "acc": None}

    def score(j):
        sj = _nt_dot(q_ref[0, _rows(qb, t), :], k_ref[0, _rows(j, t), :])
        if j == qb:
            sj = sj + bias_ref[...]
        st["s"][j] = sj
        for cb in range(n_col):
            blk = sj[:, cb * LANES:(cb + 1) * LANES]
            st["colmax"] = blk if st["colmax"] is None else jnp.maximum(st["colmax"], blk)

    def accumulate(j):
        if j == 0:
            m = jnp.broadcast_to(jnp.max(st["colmax"], axis=-1, keepdims=True), (t, LANES))
            st["m"] = jnp.concatenate([m] * n_col, axis=1)
        p = jnp.exp2(st["s"][j] - st["m"]).astype(BF16)
        pv = _dot(p, vaug_ref[_rows(j, t), :])
        st["acc"] = pv if j == 0 else st["acc"] + pv
        if j == qb:
            acc = st["acc"]
            o_ref[0, _rows(qb, t), :] = (acc[:, 0:LANES] / acc[:, LANES:2 * LANES]).astype(BF16)

    return ([lambda j=j: score(j) for j in range(qb + 1)]
            + [lambda j=j: accumulate(j) for j in range(qb + 1)])


def _merge_tasks(lists):
    keyed = [((i + 0.5) / len(tasks), n, i, task) for n, tasks in enumerate(lists) for i, task in enumerate(tasks)]
    return [task for _, _, _, task in sorted(keyed, key=lambda e: e[:3])]


def _attn_kernel(q_ref, k_ref, v_ref, tri_ref, bias_s_ref, qm_ref, km_ref, vm_ref, bias_c_ref,
                 osb_ref, omla_ref, vaug_ref):
    nq = k_ref.shape[1] // ATT_T
    vaug_ref[:, 0:LANES] = vm_ref[0]
    vaug_ref[:, LANES:2 * LANES] = jnp.ones((vm_ref.shape[1], LANES), BF16)
    order = [qb for c in range(nq // 2) for qb in (c, nq - 1 - c)]
    pending = []
    for qb in order + [None]:
        lists = list(pending)
        pending = []
        if qb is not None:
            score_tasks, weigh_tasks = _sb_tasks(qb, q_ref, k_ref, v_ref, tri_ref, bias_s_ref, osb_ref)
            lists.append(score_tasks)
            lists.append(_mla_tasks(qb, qm_ref, km_ref, vaug_ref, bias_c_ref, omla_ref))
            pending = [weigh_tasks]
        for task in _merge_tasks(lists):
            task()


def _attention(qkv, qm, km, vm, tri, bias_strict, bias_causal):
    b, s, _ = qkv.shape
    n_grp = SB_WIDTH // LANES
    const = lambda shape: pl.BlockSpec(shape, lambda bi, g: (0,) * len(shape), pipeline_mode=pl.Buffered(1))
    lane_blk = lambda w, off: pl.BlockSpec((1, s, w), lambda bi, g: (bi, 0, off + g))
    return pl.pallas_call(
        _attn_kernel,
        grid=(b, n_grp),
        in_specs=[
            lane_blk(LANES, 0), lane_blk(LANES, n_grp), lane_blk(LANES, 2 * n_grp),
            const(tri.shape), const(bias_strict.shape),
            lane_blk(_Q_HEAD_W, 0), lane_blk(_Q_HEAD_W, 0), lane_blk(LANES, 0),
            const(bias_causal.shape),
        ],
        out_specs=[lane_blk(LANES, 0), lane_blk(LANES, 0)],
        out_shape=[jax.ShapeDtypeStruct((b, s, SB_WIDTH), BF16), jax.ShapeDtypeStruct((b, s, MLA_WIDTH), BF16)],
        scratch_shapes=[pltpu.VMEM((s, 2 * LANES), BF16)],
        compiler_params=pltpu.CompilerParams(
            dimension_semantics=("arbitrary", "arbitrary"), vmem_limit_bytes=VMEM_LIMIT_BYTES),
        name="attn",
    )(qkv, qkv, qkv, tri, bias_strict, qm, km, vm, bias_causal)


def _outffn_kernel(x_ref, osb_ref, omla_ref, mod_ref, gsb_ref, gmla_ref, gffn_ref,
                   wo_ref, wg_ref, wu_ref, wd_ref, out_ref, act_ref):
    gate1 = mod_ref[0, 0:1, :]
    shift2 = mod_ref[0, 1:2, :]
    scale2 = mod_ref[0, 2:3, :]
    gate2 = mod_ref[0, 3:4, :]

    osb = osb_ref[0].astype(F32)
    omla = omla_ref[0].astype(F32)
    nsb = osb * lax.rsqrt(_mean_sq(osb, SB_WIDTH) + EPS) * gsb_ref[...]
    nmla = omla * lax.rsqrt(_mean_sq(omla, MLA_WIDTH) + EPS) * gmla_ref[...]
    mixed = jnp.concatenate([nsb.astype(BF16), nmla.astype(BF16)], axis=1)
    x1 = x_ref[0] + gate1 * _dot(mixed, wo_ref[...])

    h2 = (x1 * lax.rsqrt(_mean_sq(x1, D_MODEL) + EPS)) * (gffn_ref[...] * (1.0 + scale2)) + shift2
    h2 = h2.astype(BF16)
    d_ff = wg_ref.shape[1]
    for cidx in range(d_ff // FFN_TN):
        sl = slice(cidx * FFN_TN, (cidx + 1) * FFN_TN)
        g = _dot(h2, wg_ref[:, sl])
        u = _dot(h2, wu_ref[:, sl])
        act_ref[:, sl] = (g / (1.0 + jnp.exp(-g)) * u).astype(BF16)
    out_ref[0] = x1 + gate2 * _dot(act_ref[...], wd_ref[...])


def _outffn(x, osb, omla, mod3, gsb, gmla, gffn, wo, wg, wu, wd):
    b, s, d = x.shape
    tm = FFN_TM
    d_ff = wg.shape[1]
    const = lambda shape: pl.BlockSpec(shape, lambda i, j: (0,) * len(shape), pipeline_mode=pl.Buffered(1))
    tok = lambda w: pl.BlockSpec((1, tm, w), lambda i, j: (i, j, 0))
    return pl.pallas_call(
        _outffn_kernel,
        grid=(b, s // tm),
        in_specs=[
            tok(d), tok(SB_WIDTH), tok(MLA_WIDTH),
            pl.BlockSpec((1, mod3.shape[1], d), lambda i, j: (i, 0, 0)),
            const((1, SB_WIDTH)), const((1, MLA_WIDTH)), const((1, d)),
            const(wo.shape), const(wg.shape), const(wu.shape), const(wd.shape),
        ],
        out_specs=tok(d),
        out_shape=jax.ShapeDtypeStruct((b, s, d), F32),
        scratch_shapes=[pltpu.VMEM((tm, d_ff), BF16)],
        compiler_params=pltpu.CompilerParams(
            dimension_semantics=("arbitrary", "arbitrary"), vmem_limit_bytes=VMEM_LIMIT_BYTES),
        name="outffn",
    )(x, osb, omla, mod3, gsb, gmla, gffn, wo, wg, wu, wd)


def _swap_halves(a):
    half = a.shape[-1] // 2
    return jnp.concatenate([a[..., half:], a[..., :half]], axis=-1)


def _layer_params(l, w_q_up, w_kv_up, q_norm, k_rope_norm):
    d_q = w_q_up.shape[1]
    wq4 = w_q_up[l].reshape(d_q, MLA_HEADS, MLA_QK_DIM)
    wq = jnp.concatenate([wq4, _swap_halves(wq4[:, :, MLA_NOPE_DIM:])], axis=2)
    wq = wq.reshape(d_q, MLA_HEADS * _Q_HEAD_W).astype(BF16)

    wkv = w_kv_up[l].astype(BF16)

    gqn = q_norm[l][:MLA_NOPE_DIM].reshape(1, LANES)
    gr = q_norm[l][MLA_NOPE_DIM:]
    gqr = jnp.concatenate([gr, _swap_halves(gr)]).reshape(1, LANES)
    gk = k_rope_norm[l]
    gk1 = jnp.concatenate([gk, gk]).reshape(1, LANES)
    gk2 = jnp.concatenate([_swap_halves(gk), _swap_halves(gk)]).reshape(1, LANES)
    return wq, wkv, gqn, gqr, gk1, gk2


def _constants():
    half = MLA_ROPE_DIM // 2
    freqs = 1.0 / (ROPE_THETA ** (jnp.arange(half, dtype=F32) / half))
    freq = jnp.tile(freqs, LANES // half).reshape(1, LANES)
    sign = np.tile(np.concatenate([-np.ones(half), np.ones(half)]), LANES // (2 * half))
    sign = jnp.asarray(sign, F32).reshape(1, LANES)
    t = ATT_T
    row = np.arange(t)[:, None]
    col = np.arange(t)[None, :]
    tri = jnp.asarray((row >= col).astype(np.float32), BF16)
    bias_strict = jnp.asarray(np.where(col < row, 0.0, MASK_BIAS), F32)
    bias_causal = jnp.asarray(np.where(col <= row, 0.0, MASK_BIAS), F32)
    return freq, sign, tri, bias_strict, bias_causal


def kernel(x, c, positions, w_ada, b_ada, norm_attn, norm_ffn, w_in, q_a_norm, w_q_up, kv_a_norm, w_kv_up,
           q_norm, k_nope_norm, k_rope_norm, out_norm_sb, out_norm_mla, w_out, w_gate, w_up, w_down):
    b, s, d = x.shape
    depth = w_ada.shape[0]
    freq, sign, tri, bias_strict, bias_causal = _constants()
    n_grp = LANES // (MLA_ROPE_DIM // 2)
    rows = PROJ_TS // n_grp
    pos4 = positions.astype(F32).reshape(b, s // PROJ_TS, n_grp, rows).transpose(0, 1, 3, 2)
    pos4 = jnp.repeat(pos4, LANES // n_grp, axis=-1)
    for l in range(depth):
        b_ada2 = b_ada[l].reshape(1, -1)
        mod_early = _adaln(c, w_ada[l], b_ada2, N_MOD_EARLY * d).reshape(b, N_MOD_EARLY, d)
        wq, wkv, gqn, gqr, gk1, gk2 = _layer_params(l, w_q_up, w_kv_up, q_norm, k_rope_norm)
        qkv, qm, km, vm, wo, wg, wu, wd, mod_late = _inproj(
            x, pos4, mod_early, norm_attn[l].reshape(1, d), w_in[l], q_a_norm[l].reshape(1, -1), wq,
            kv_a_norm[l].reshape(1, -1), wkv, gqn, gqr, k_nope_norm[l].reshape(1, LANES), gk1, gk2, freq, sign,
            (w_out[l], w_gate[l], w_up[l], w_down[l]), c, w_ada[l], b_ada2)
        osb, omla = _attention(qkv, qm, km, vm, tri, bias_strict, bias_causal)
        x = _outffn(x, osb, omla, mod_late.reshape(b, N_MOD - N_MOD_EARLY, d), out_norm_sb[l].reshape(1, -1),
                    out_norm_mla[l].reshape(1, -1), norm_ffn[l].reshape(1, d), wo, wg, wu, wd)
    return x
```

```python
import numpy as np
import jax
import jax.numpy as jnp
from jax import lax
from jax.experimental import pallas as pl
from jax.experimental.pallas import tpu as pltpu

F32 = jnp.float32
BF16 = jnp.bfloat16

D_MODEL = 1024
SB_HEADS = 8
SB_HEAD_DIM = 64
SB_WIDTH = SB_HEADS * SB_HEAD_DIM
MLA_HEADS = 4
MLA_NOPE_DIM = 128
MLA_ROPE_DIM = 64
MLA_QK_DIM = MLA_NOPE_DIM + MLA_ROPE_DIM
MLA_V_DIM = 128
MLA_Q_RANK = 384
MLA_KV_RANK = 256
MLA_WIDTH = MLA_HEADS * MLA_V_DIM
N_MOD = 6
N_MOD_EARLY = 2
ROPE_THETA = 10000.0
EPS = 1e-6
LOG2E = float(np.log2(np.e))

LANES = 128
BF16_SUBLANES = 16
VMEM_LIMIT_BYTES = 56 * 1024 * 1024

ADALN_TN = 1024
PROJ_TS = 512
ATT_T = 256
FFN_TM = 512
FFN_TN = 256
MASK_BIAS = -1e9
SOFTPLUS_LINEAR_ABOVE = 30.0

_C_QSB = 0
_C_KSB = SB_WIDTH
_C_VSB = 2 * SB_WIDTH
_C_CQ = 3 * SB_WIDTH
_C_CKV = _C_CQ + MLA_Q_RANK
_C_KR = _C_CKV + MLA_KV_RANK
_N_IN = _C_KR + MLA_ROPE_DIM
_N_PROJ = _C_KR + LANES
_CAST_ROWS = 14 * BF16_SUBLANES
assert _N_IN % _CAST_ROWS == 0
_Q_HEAD_W = 2 * LANES
assert MLA_NOPE_DIM == LANES and MLA_V_DIM == LANES and 2 * MLA_ROPE_DIM == LANES


def _nt_dot(a, b):
    return lax.dot_general(a, b, (((1,), (1,)), ((), ())), preferred_element_type=F32)


def _dot(a, b):
    return jnp.dot(a, b, preferred_element_type=F32)


def _mean_sq(x, n):
    return jnp.sum(x * x, axis=-1, keepdims=True) * (1.0 / n)


def _adaln_columns(c, w, bias):
    sc = c / (1.0 + jnp.exp(-c))
    s_hi = sc.astype(BF16)
    s_lo = (sc - s_hi.astype(F32)).astype(BF16)
    w_hi = w.astype(BF16)
    w_lo = (w - w_hi.astype(F32)).astype(BF16)
    return _dot(s_hi, w_hi) + _dot(s_lo, w_hi) + _dot(s_hi, w_lo) + bias


def _adaln_kernel(c_ref, w_ref, b_ref, o_ref):
    o_ref[...] = _adaln_columns(c_ref[...], w_ref[...], b_ref[...])


def _adaln(c, w_ada, b_ada2, n_cols):
    b, d = c.shape
    return pl.pallas_call(
        _adaln_kernel,
        grid=(n_cols // ADALN_TN,),
        in_specs=[
            pl.BlockSpec((b, d), lambda j: (0, 0)),
            pl.BlockSpec((d, ADALN_TN), lambda j: (0, j)),
            pl.BlockSpec((1, ADALN_TN), lambda j: (0, j)),
        ],
        out_specs=pl.BlockSpec((b, ADALN_TN), lambda j: (0, j)),
        out_shape=jax.ShapeDtypeStruct((b, n_cols), F32),
        compiler_params=pltpu.CompilerParams(
            dimension_semantics=("arbitrary",), vmem_limit_bytes=VMEM_LIMIT_BYTES),
        name="adaln",
    )(c, w_ada, b_ada2)


def _rope_tables(pos4, freq):
    half = MLA_ROPE_DIM // 2
    n_grp = LANES // half
    grp = lax.shift_right_logical(lax.broadcasted_iota(jnp.int32, (1, LANES), 1), int(np.log2(half)))
    ang4 = pos4 * freq

    def expand(x4):
        blocks = []
        for g in range(n_grp):
            xg = jnp.where(grp == g, x4, 0.0)
            y = xg
            for k in range(1, n_grp):
                y = y + pltpu.roll(xg, k * half, axis=1)
            blocks.append(y)
        return jnp.concatenate(blocks, axis=0)

    return expand(jnp.cos(ang4)), expand(jnp.sin(ang4))


def _inproj_kernel(x_ref, pos4_ref, mod_ref, gattn_ref, win_ref, gqa_ref, wq_ref, gkva_ref, wkv_ref,
                   gqn_ref, gqr_ref, gkn_ref, gk1_ref, gk2_ref, freq_ref, sign_ref,
                   wo_ref, wg_ref, wu_ref, wd_ref, c_ref, wada_ref, bada_ref,
                   qkv_ref, qm_ref, km_ref, vm_ref, wo_bf_ref, wg_bf_ref, wu_bf_ref, wd_bf_ref, mod_late_ref,
                   w1_ref):
    @pl.when(jnp.logical_and(pl.program_id(0) == 0, pl.program_id(1) == 0))
    def _():
        for r0 in range(0, _N_IN, _CAST_ROWS):
            w1_ref[r0:r0 + _CAST_ROWS, :] = win_ref[r0:r0 + _CAST_ROWS, :].astype(BF16)
        w1_ref[_N_IN:_N_PROJ, :] = jnp.zeros((_N_PROJ - _N_IN, w1_ref.shape[1]), BF16)

    cos_t, sin_t = _rope_tables(pos4_ref[0, 0], freq_ref[...])
    sin_t = sin_t * sign_ref[...]

    x = x_ref[0]
    shift1 = mod_ref[0, 0:1, :]
    scale1 = mod_ref[0, 1:2, :]
    r = lax.rsqrt(_mean_sq(x, D_MODEL) + EPS)
    h = ((x * r) * (gattn_ref[...] * (1.0 + scale1)) + shift1).astype(BF16)
    proj = _nt_dot(h, w1_ref[_C_CQ:_N_PROJ, :])
    o_cq, o_ckv, o_kr = 0, _C_CKV - _C_CQ, _C_KR - _C_CQ

    cq = proj[:, o_cq:o_ckv]
    cqn = cq * lax.rsqrt(_mean_sq(cq, MLA_Q_RANK) + EPS) * gqa_ref[...]
    y = _dot(cqn.astype(BF16), wq_ref[...])
    lane = lax.broadcasted_iota(jnp.int32, (1, LANES), 1)
    rope_q = jnp.where(lane < MLA_ROPE_DIM, cos_t, sin_t) * gqr_ref[...]
    for hd in range(MLA_HEADS):
        yn = y[:, hd * _Q_HEAD_W: hd * _Q_HEAD_W + LANES]
        yr = y[:, hd * _Q_HEAD_W + LANES: (hd + 1) * _Q_HEAD_W]
        ssq = jnp.sum(yn * yn, axis=-1, keepdims=True) + 0.5 * jnp.sum(yr * yr, axis=-1, keepdims=True)
        rq = lax.rsqrt(ssq * (1.0 / MLA_QK_DIM) + EPS) * (MLA_QK_DIM ** -0.5 * LOG2E)
        qm_ref[0, :, hd * _Q_HEAD_W: hd * _Q_HEAD_W + LANES] = (yn * rq * gqn_ref[...]).astype(BF16)
        qm_ref[0, :, hd * _Q_HEAD_W + LANES: (hd + 1) * _Q_HEAD_W] = (yr * rq * rope_q).astype(BF16)

    kr0 = proj[:, o_kr:o_kr + LANES]
    z1 = kr0 + pltpu.roll(kr0, MLA_ROPE_DIM, axis=1)
    z2 = pltpu.roll(z1, MLA_ROPE_DIM // 2, axis=1)
    rk = lax.rsqrt(_mean_sq(z1, LANES) + EPS)
    k_rope = (rk * (z1 * (gk1_ref[...] * cos_t) + z2 * (gk2_ref[...] * sin_t))).astype(BF16)

    ckv = proj[:, o_ckv:o_kr]
    ckvn = ckv * lax.rsqrt(_mean_sq(ckv, MLA_KV_RANK) + EPS) * gkva_ref[...]
    kv = _dot(ckvn.astype(BF16), wkv_ref[...])
    for hd in range(MLA_HEADS):
        kn = kv[:, hd * _Q_HEAD_W: hd * _Q_HEAD_W + LANES]
        km_ref[0, :, hd * _Q_HEAD_W: hd * _Q_HEAD_W + LANES] = (
            kn * lax.rsqrt(_mean_sq(kn, MLA_NOPE_DIM) + EPS) * gkn_ref[...]).astype(BF16)
        km_ref[0, :, hd * _Q_HEAD_W + LANES: (hd + 1) * _Q_HEAD_W] = k_rope
        vm_ref[0, :, hd * LANES:(hd + 1) * LANES] = kv[:, hd * _Q_HEAD_W + LANES: (hd + 1) * _Q_HEAD_W].astype(BF16)

    proj_sb = _nt_dot(h, w1_ref[_C_QSB:_C_CQ, :])
    qkv_ref[0, :, 0:SB_WIDTH] = (proj_sb[:, _C_QSB:_C_KSB] * (SB_HEAD_DIM ** -0.5)).astype(BF16)
    qkv_ref[0, :, SB_WIDTH:3 * SB_WIDTH] = proj_sb[:, _C_KSB:_C_CQ].astype(BF16)

    for src, dst in ((wo_ref, wo_bf_ref), (wg_ref, wg_bf_ref), (wu_ref, wu_bf_ref), (wd_ref, wd_bf_ref)):
        dst[...] = src[...].astype(BF16)
    mod_late_ref[...] = _adaln_columns(c_ref[...], wada_ref[...], bada_ref[...])


def _inproj(x, pos4, mod3, gattn, w_in, gqa, wq, gkva, wkv, gqn, gqr, gkn, gk1, gk2, freq, sign, later_weights,
            c, w_ada, b_ada2):
    b, s, d = x.shape
    ts = PROJ_TS
    n_tok = s // ts
    n_early = mod3.shape[1] * d
    late_cols = (w_ada.shape[1] - n_early) // (b * n_tok)
    assert late_cols % LANES == 0 and late_cols * b * n_tok == w_ada.shape[1] - n_early
    late = lambda rows: pl.BlockSpec((rows, late_cols), lambda i, j: (0, n_early // late_cols + i * n_tok + j))
    const = lambda shape: pl.BlockSpec(shape, lambda i, j: (0,) * len(shape), pipeline_mode=pl.Buffered(1))
    tok = lambda w: pl.BlockSpec((1, ts, w), lambda i, j: (i, j, 0))

    def cast_slice(w):
        rows = w.shape[0]
        n_steps = b * n_tok
        while rows % (n_steps * BF16_SUBLANES):
            n_steps //= 2
        per = (b * n_tok) // n_steps
        return pl.BlockSpec((rows // n_steps, w.shape[1]), lambda i, j: ((i * n_tok + j) // per, 0))

    cast_specs = [cast_slice(w) for w in later_weights]
    return pl.pallas_call(
        _inproj_kernel,
        grid=(b, s // ts),
        in_specs=[
            tok(d),
            pl.BlockSpec((1, 1, ts // (LANES // (MLA_ROPE_DIM // 2)), LANES), lambda i, j: (i, j, 0, 0)),
            pl.BlockSpec((1, mod3.shape[1], d), lambda i, j: (i, 0, 0)),
            const((1, d)),
            const(w_in.shape),
            const((1, MLA_Q_RANK)),
            const(wq.shape),
            const((1, MLA_KV_RANK)),
            const(wkv.shape),
            const((1, LANES)), const((1, LANES)), const((1, LANES)), const((1, LANES)), const((1, LANES)),
            const((1, LANES)), const((1, LANES)),
        ] + cast_specs + [const(c.shape), late(d), late(1)],
        out_specs=[tok(3 * SB_WIDTH), tok(MLA_HEADS * _Q_HEAD_W), tok(MLA_HEADS * _Q_HEAD_W), tok(MLA_WIDTH)]
        + cast_specs + [pl.BlockSpec((b, late_cols), lambda i, j: (0, i * n_tok + j))],
        out_shape=[
            jax.ShapeDtypeStruct((b, s, 3 * SB_WIDTH), BF16),
            jax.ShapeDtypeStruct((b, s, MLA_HEADS * _Q_HEAD_W), BF16),
            jax.ShapeDtypeStruct((b, s, MLA_HEADS * _Q_HEAD_W), BF16),
            jax.ShapeDtypeStruct((b, s, MLA_WIDTH), BF16),
        ] + [jax.ShapeDtypeStruct(w.shape, BF16) for w in later_weights]
        + [jax.ShapeDtypeStruct((b, w_ada.shape[1] - n_early), F32)],
        scratch_shapes=[pltpu.VMEM((_N_PROJ, d), BF16)],
        compiler_params=pltpu.CompilerParams(
            dimension_semantics=("arbitrary", "arbitrary"), vmem_limit_bytes=VMEM_LIMIT_BYTES),
        name="inproj",
    )(x, pos4, mod3, gattn, w_in, gqa, wq, gkva, wkv, gqn, gqr, gkn, gk1, gk2, freq, sign, *later_weights,
      c, w_ada, b_ada2)


def _rows(j, t):
    return pl.ds(j * t, t)


def _sb_tasks(qb, q_ref, k_ref, v_ref, tri_ref, bias_ref, o_ref):
    t = ATT_T
    first = lax.broadcasted_iota(jnp.int32, (1, LANES), 1) < SB_HEAD_DIM
    q = q_ref[0, _rows(qb, t), :]
    qh = (jnp.where(first, q, jnp.zeros_like(q)), jnp.where(first, jnp.zeros_like(q), q))

    def softplus_bf16(zz):
        soft = jnp.log(1.0 + jnp.exp2(zz * LOG2E))
        return jnp.where(zz > SOFTPLUS_LINEAR_ABOVE, zz, soft).astype(BF16)

    half = t // 2

    def quadrants(full):
        return full[0:half, 0:half], full[half:t, :]

    def assemble(top_left, bottom):
        top = jnp.concatenate([top_left, jnp.zeros((half, t - half), top_left.dtype)], axis=1)
        return jnp.concatenate([top, bottom], axis=0)

    heads = (0, 1)
    q_both = jnp.concatenate(qh, axis=0)
    z, sp = {}, {}

    def score(j):
        zz_both = _nt_dot(q_both, k_ref[0, _rows(j, t), :])
        parts = []
        for hh in heads:
            zz = zz_both[hh * t:(hh + 1) * t]
            if j == qb:
                z[j, hh] = tuple(a + b for a, b in zip(quadrants(zz), quadrants(bias_ref[...])))
                parts.append(assemble(*(softplus_bf16(part) for part in z[j, hh])))
            else:
                z[j, hh] = zz
                parts.append(softplus_bf16(zz))
        sp[j] = jnp.concatenate(parts, axis=0)

    run, acc = {}, {}

    def weigh(j):
        cin_both = _dot(sp[j], tri_ref[...])
        ws = []
        for hh in heads:
            cin = cin_both[hh * t:(hh + 1) * t]
            if j == qb:
                ws.append(assemble(*(jnp.exp2((zp - cp) * LOG2E).astype(BF16)
                                     for zp, cp in zip(z[j, hh], quadrants(cin)))))
            else:
                arg = z[j, hh] - cin - jnp.concatenate([run[hh]] * (t // LANES), axis=1)
                ws.append(jnp.exp2(arg * LOG2E).astype(BF16))
            tot = jnp.broadcast_to(cin[:, 0:1], (t, LANES))
            run[hh] = tot if j == qb else run[hh] + tot
        pv_both = _dot(jnp.concatenate(ws, axis=0), v_ref[0, _rows(j, t), :])
        for hh in heads:
            pv = pv_both[hh * t:(hh + 1) * t]
            acc[hh] = pv if j == qb else acc[hh] + pv
        if j == 0:
            o_ref[0, _rows(qb, t), :] = jnp.where(first, acc[0], acc[1]).astype(BF16)

    score_tasks = [lambda j=j: score(j) for j in range(qb + 1)]
    weigh_tasks = [lambda j=j: weigh(j) for j in range(qb, -1, -1)]
    return score_tasks, weigh_tasks


def _mla_tasks(qb, q_ref, k_ref, vaug_ref, bias_ref, o_ref):
    t = ATT_T
    n_col = t // LANES
    st = {"s": {}, "colmax": None, "acc": None}

    def score(j):
        sj = _nt_dot(q_ref[0, _rows(qb, t), :], k_ref[0, _rows(j, t), :])
        if j == qb:
            sj = sj + bias_ref[...]
        st["s"][j] = sj
        for cb in range(n_col):
            blk = sj[:, cb * LANES:(cb + 1) * LANES]
            st["colmax"] = blk if st["colmax"] is None else jnp.maximum(st["colmax"], blk)

    def accumulate(j):
        if j == 0:
            m = jnp.broadcast_to(jnp.max(st["colmax"], axis=-1, keepdims=True), (t, LANES))
            st["m"] = jnp.concatenate([m] * n_col, axis=1)
        p = jnp.exp2(st["s"][j] - st["m"]).astype(BF16)
        pv = _dot(p, vaug_ref[_rows(j, t), :])
        st["acc"] = pv if j == 0 else st["acc"] + pv
        if j == qb:
            acc = st["acc"]
            o_ref[0, _rows(qb, t), :] = (acc[:, 0:LANES] / acc[:, LANES:2 * LANES]).astype(BF16)

    return ([lambda j=j: score(j) for j in range(qb + 1)]
            + [lambda j=j: accumulate(j) for j in range(qb + 1)])


def _merge_tasks(lists):
    keyed = [((i + 0.5) / len(tasks), n, i, task) for n, tasks in enumerate(lists) for i, task in enumerate(tasks)]
    return [task for _, _, _, task in sorted(keyed, key=lambda e: e[:3])]


def _attn_kernel(q_ref, k_ref, v_ref, tri_ref, bias_s_ref, qm_ref, km_ref, vm_ref, bias_c_ref,
                 osb_ref, omla_ref, vaug_ref):
    nq = k_ref.shape[1] // ATT_T
    vaug_ref[:, 0:LANES] = vm_ref[0]
    vaug_ref[:, LANES:2 * LANES] = jnp.ones((vm_ref.shape[1], LANES), BF16)
    order = [qb for c in range(nq // 2) for qb in (c, nq - 1 - c)]
    pending = []
    for qb in order + [None]:
        lists = list(pending)
        pending = []
        if qb is not None:
            score_tasks, weigh_tasks = _sb_tasks(qb, q_ref, k_ref, v_ref, tri_ref, bias_s_ref, osb_ref)
            lists.append(score_tasks)
            lists.append(_mla_tasks(qb, qm_ref, km_ref, vaug_ref, bias_c_ref, omla_ref))
            pending = [weigh_tasks]
        for task in _merge_tasks(lists):
            task()


def _attention(qkv, qm, km, vm, tri, bias_strict, bias_causal):
    b, s, _ = qkv.shape
    n_grp = SB_WIDTH // LANES
    const = lambda shape: pl.BlockSpec(shape, lambda bi, g: (0,) * len(shape), pipeline_mode=pl.Buffered(1))
    lane_blk = lambda w, off: pl.BlockSpec((1, s, w), lambda bi, g: (bi, 0, off + g))
    return pl.pallas_call(
        _attn_kernel,
        grid=(b, n_grp),
        in_specs=[
            lane_blk(LANES, 0), lane_blk(LANES, n_grp), lane_blk(LANES, 2 * n_grp),
            const(tri.shape), const(bias_strict.shape),
            lane_blk(_Q_HEAD_W, 0), lane_blk(_Q_HEAD_W, 0), lane_blk(LANES, 0),
            const(bias_causal.shape),
        ],
        out_specs=[lane_blk(LANES, 0), lane_blk(LANES, 0)],
        out_shape=[jax.ShapeDtypeStruct((b, s, SB_WIDTH), BF16), jax.ShapeDtypeStruct((b, s, MLA_WIDTH), BF16)],
        scratch_shapes=[pltpu.VMEM((s, 2 * LANES), BF16)],
        compiler_params=pltpu.CompilerParams(
            dimension_semantics=("arbitrary", "arbitrary"), vmem_limit_bytes=VMEM_LIMIT_BYTES),
        name="attn",
    )(qkv, qkv, qkv, tri, bias_strict, qm, km, vm, bias_causal)


def _outffn_kernel(x_ref, osb_ref, omla_ref, mod_ref, gsb_ref, gmla_ref, gffn_ref,
                   wo_ref, wg_ref, wu_ref, wd_ref, out_ref, act_ref):
    gate1 = mod_ref[0, 0:1, :]
    shift2 = mod_ref[0, 1:2, :]
    scale2 = mod_ref[0, 2:3, :]
    gate2 = mod_ref[0, 3:4, :]

    osb = osb_ref[0].astype(F32)
    omla = omla_ref[0].astype(F32)
    nsb = osb * lax.rsqrt(_mean_sq(osb, SB_WIDTH) + EPS) * gsb_ref[...]
    nmla = omla * lax.rsqrt(_mean_sq(omla, MLA_WIDTH) + EPS) * gmla_ref[...]
    mixed = jnp.concatenate([nsb.astype(BF16), nmla.astype(BF16)], axis=1)
    x1 = x_ref[0] + gate1 * _dot(mixed, wo_ref[...])

    h2 = (x1 * lax.rsqrt(_mean_sq(x1, D_MODEL) + EPS)) * (gffn_ref[...] * (1.0 + scale2)) + shift2
    h2 = h2.astype(BF16)
    d_ff = wg_ref.shape[1]
    for cidx in range(d_ff // FFN_TN):
        sl = slice(cidx * FFN_TN, (cidx + 1) * FFN_TN)
        g = _dot(h2, wg_ref[:, sl])
        u = _dot(h2, wu_ref[:, sl])
        act_ref[:, sl] = (g / (1.0 + jnp.exp(-g)) * u).astype(BF16)
    out_ref[0] = x1 + gate2 * _dot(act_ref[...], wd_ref[...])


def _outffn(x, osb, omla, mod3, gsb, gmla, gffn, wo, wg, wu, wd):
    b, s, d = x.shape
    tm = FFN_TM
    d_ff = wg.shape[1]
    const = lambda shape: pl.BlockSpec(shape, lambda i, j: (0,) * len(shape), pipeline_mode=pl.Buffered(1))
    tok = lambda w: pl.BlockSpec((1, tm, w), lambda i, j: (i, j, 0))
    return pl.pallas_call(
        _outffn_kernel,
        grid=(b, s // tm),
        in_specs=[
            tok(d), tok(SB_WIDTH), tok(MLA_WIDTH),
            pl.BlockSpec((1, mod3.shape[1], d), lambda i, j: (i, 0, 0)),
            const((1, SB_WIDTH)), const((1, MLA_WIDTH)), const((1, d)),
            const(wo.shape), const(wg.shape), const(wu.shape), const(wd.shape),
        ],
        out_specs=tok(d),
        out_shape=jax.ShapeDtypeStruct((b, s, d), F32),
        scratch_shapes=[pltpu.VMEM((tm, d_ff), BF16)],
        compiler_params=pltpu.CompilerParams(
            dimension_semantics=("arbitrary", "arbitrary"), vmem_limit_bytes=VMEM_LIMIT_BYTES),
        name="outffn",
    )(x, osb, omla, mod3, gsb, gmla, gffn, wo, wg, wu, wd)


def _swap_halves(a):
    half = a.shape[-1] // 2
    return jnp.concatenate([a[..., half:], a[..., :half]], axis=-1)


def _layer_params(l, w_q_up, w_kv_up, q_norm, k_rope_norm):
    d_q = w_q_up.shape[1]
    wq4 = w_q_up[l].reshape(d_q, MLA_HEADS, MLA_QK_DIM)
    wq = jnp.concatenate([wq4, _swap_halves(wq4[:, :, MLA_NOPE_DIM:])], axis=2)
    wq = wq.reshape(d_q, MLA_HEADS * _Q_HEAD_W).astype(BF16)

    wkv = w_kv_up[l].astype(BF16)

    gqn = q_norm[l][:MLA_NOPE_DIM].reshape(1, LANES)
    gr = q_norm[l][MLA_NOPE_DIM:]
    gqr = jnp.concatenate([gr, _swap_halves(gr)]).reshape(1, LANES)
    gk = k_rope_norm[l]
    gk1 = jnp.concatenate([gk, gk]).reshape(1, LANES)
    gk2 = jnp.concatenate([_swap_halves(gk), _swap_halves(gk)]).reshape(1, LANES)
    return wq, wkv, gqn, gqr, gk1, gk2


def _constants():
    half = MLA_ROPE_DIM // 2
    freqs = 1.0 / (ROPE_THETA ** (jnp.arange(half, dtype=F32) / half))
    freq = jnp.tile(freqs, LANES // half).reshape(1, LANES)
    sign = np.tile(np.concatenate([-np.ones(half), np.ones(half)]), LANES // (2 * half))
    sign = jnp.asarray(sign, F32).reshape(1, LANES)
    t = ATT_T
    row = np.arange(t)[:, None]
    col = np.arange(t)[None, :]
    tri = jnp.asarray((row >= col).astype(np.float32), BF16)
    bias_strict = jnp.asarray(np.where(col < row, 0.0, MASK_BIAS), F32)
    bias_causal = jnp.asarray(np.where(col <= row, 0.0, MASK_BIAS), F32)
    return freq, sign, tri, bias_strict, bias_causal


def kernel(x, c, positions, w_ada, b_ada, norm_attn, norm_ffn, w_in, q_a_norm, w_q_up, kv_a_norm, w_kv_up,
           q_norm, k_nope_norm, k_rope_norm, out_norm_sb, out_norm_mla, w_out, w_gate, w_up, w_down):
    b, s, d = x.shape
    depth = w_ada.shape[0]
    freq, sign, tri, bias_strict, bias_causal = _constants()
    n_grp = LANES // (MLA_ROPE_DIM // 2)
    rows = PROJ_TS // n_grp
    pos4 = positions.astype(F32).reshape(b, s // PROJ_TS, n_grp, rows).transpose(0, 1, 3, 2)
    pos4 = jnp.repeat(pos4, LANES // n_grp, axis=-1)
    for l in range(depth):
        b_ada2 = b_ada[l].reshape(1, -1)
        mod_early = _adaln(c, w_ada[l], b_ada2, N_MOD_EARLY * d).reshape(b, N_MOD_EARLY, d)
        wq, wkv, gqn, gqr, gk1, gk2 = _layer_params(l, w_q_up, w_kv_up, q_norm, k_rope_norm)
        qkv, qm, km, vm, wo, wg, wu, wd, mod_late = _inproj(
            x, pos4, mod_early, norm_attn[l].reshape(1, d), jnp.swapaxes(w_in[l], 0, 1), q_a_norm[l].reshape(1, -1), wq,
            kv_a_norm[l].reshape(1, -1), wkv, gqn, gqr, k_nope_norm[l].reshape(1, LANES), gk1, gk2, freq, sign,
            (w_out[l], w_gate[l], w_up[l], w_down[l]), c, w_ada[l], b_ada2)
        osb, omla = _attention(qkv, qm, km, vm, tri, bias_strict, bias_causal)
        x = _outffn(x, osb, omla, mod_late.reshape(b, N_MOD - N_MOD_EARLY, d), out_norm_sb[l].reshape(1, -1),
                    out_norm_mla[l].reshape(1, -1), norm_ffn[l].reshape(1, d), wo, wg, wu, wd)
    return x
```

```python
import numpy as np
import jax
import jax.numpy as jnp
from jax import lax
from jax.experimental import pallas as pl
from jax.experimental.pallas import tpu as pltpu

F32 = jnp.float32
BF16 = jnp.bfloat16

D_MODEL = 1024
SB_HEADS = 8
SB_HEAD_DIM = 64
SB_WIDTH = SB_HEADS * SB_HEAD_DIM
MLA_HEADS = 4
MLA_NOPE_DIM = 128
MLA_ROPE_DIM = 64
MLA_QK_DIM = MLA_NOPE_DIM + MLA_ROPE_DIM
MLA_V_DIM = 128
MLA_Q_RANK = 384
MLA_KV_RANK = 256
MLA_WIDTH = MLA_HEADS * MLA_V_DIM
N_MOD = 6
N_MOD_EARLY = 2
ROPE_THETA = 10000.0
EPS = 1e-6
LOG2E = float(np.log2(np.e))

LANES = 128
BF16_SUBLANES = 16
VMEM_LIMIT_BYTES = 56 * 1024 * 1024

ADALN_TN = 1024
PROJ_TS = 512
ATT_T = 256
FFN_TM = 512
FFN_TN = 256
MASK_BIAS = -1e9
SOFTPLUS_LINEAR_ABOVE = 30.0

_C_QSB = 0
_C_KSB = SB_WIDTH
_C_CQ = 3 * SB_WIDTH
_C_CKV = _C_CQ + MLA_Q_RANK
_C_KR = _C_CKV + MLA_KV_RANK
_N_IN = _C_KR + MLA_ROPE_DIM
_N_PROJ = _C_KR + LANES
_CAST_ROWS = 14 * BF16_SUBLANES
assert _N_IN % _CAST_ROWS == 0
_Q_HEAD_W = 2 * LANES
assert MLA_NOPE_DIM == LANES and MLA_V_DIM == LANES and 2 * MLA_ROPE_DIM == LANES


def _nt_dot(a, b):
    return lax.dot_general(a, b, (((1,), (1,)), ((), ())), preferred_element_type=F32)


def _dot(a, b):
    return jnp.dot(a, b, preferred_element_type=F32)


def _mean_sq(x, n):
    return jnp.sum(x * x, axis=-1, keepdims=True) * (1.0 / n)


def _adaln_columns(c, w, bias):
    sc = c / (1.0 + jnp.exp(-c))
    s_hi = sc.astype(BF16)
    s_lo = (sc - s_hi.astype(F32)).astype(BF16)
    w_hi = w.astype(BF16)
    w_lo = (w - w_hi.astype(F32)).astype(BF16)
    return _dot(s_hi, w_hi) + _dot(s_lo, w_hi) + _dot(s_hi, w_lo) + bias


def _adaln_kernel(c_ref, w_ref, b_ref, o_ref):
    o_ref[...] = _adaln_columns(c_ref[...], w_ref[...], b_ref[...])


def _adaln(c, w_ada, b_ada2, n_cols):
    b, d = c.shape
    return pl.pallas_call(
        _adaln_kernel,
        grid=(n_cols // ADALN_TN,),
        in_specs=[
            pl.BlockSpec((b, d), lambda j: (0, 0)),
            pl.BlockSpec((d, ADALN_TN), lambda j: (0, j)),
            pl.BlockSpec((1, ADALN_TN), lambda j: (0, j)),
        ],
        out_specs=pl.BlockSpec((b, ADALN_TN), lambda j: (0, j)),
        out_shape=jax.ShapeDtypeStruct((b, n_cols), F32),
        compiler_params=pltpu.CompilerParams(
            dimension_semantics=("arbitrary",), vmem_limit_bytes=VMEM_LIMIT_BYTES),
        name="adaln",
    )(c, w_ada, b_ada2)


def _rope_tables(pos4, freq):
    half = MLA_ROPE_DIM // 2
    n_grp = LANES // half
    grp = lax.shift_right_logical(lax.broadcasted_iota(jnp.int32, (1, LANES), 1), int(np.log2(half)))
    ang4 = pos4 * freq

    def expand(x4):
        blocks = []
        for g in range(n_grp):
            xg = jnp.where(grp == g, x4, 0.0)
            y = xg
            for k in range(1, n_grp):
                y = y + pltpu.roll(xg, k * half, axis=1)
            blocks.append(y)
        return jnp.concatenate(blocks, axis=0)

    return expand(jnp.cos(ang4)), expand(jnp.sin(ang4))


def _inproj_kernel(x_ref, pos4_ref, mod_ref, gattn_ref, win_ref, gqa_ref, wq_ref, gkva_ref, wkv_ref,
                   gqn_ref, gqr_ref, gkn_ref, gk1_ref, gk2_ref, freq_ref, sign_ref,
                   wo_ref, wg_ref, wu_ref, wd_ref, c_ref, wada_ref, bada_ref,
                   qkv_ref, qm_ref, km_ref, vm_ref, wo_bf_ref, wg_bf_ref, wu_bf_ref, wd_bf_ref, mod_late_ref,
                   w1_ref):
    @pl.when(jnp.logical_and(pl.program_id(0) == 0, pl.program_id(1) == 0))
    def _():
        for r0 in range(0, _N_IN, _CAST_ROWS):
            w1_ref[r0:r0 + _CAST_ROWS, :] = win_ref[r0:r0 + _CAST_ROWS, :].astype(BF16)
        w1_ref[_N_IN:_N_PROJ, :] = jnp.zeros((_N_PROJ - _N_IN, w1_ref.shape[1]), BF16)

    cos_t, sin_t = _rope_tables(pos4_ref[0, 0], freq_ref[...])
    sin_t = sin_t * sign_ref[...]

    x = x_ref[0]
    shift1 = mod_ref[0, 0:1, :]
    scale1 = mod_ref[0, 1:2, :]
    r = lax.rsqrt(_mean_sq(x, D_MODEL) + EPS)
    h = ((x * r) * (gattn_ref[...] * (1.0 + scale1)) + shift1).astype(BF16)
    proj = _nt_dot(h, w1_ref[_C_CQ:_N_PROJ, :])
    o_cq, o_ckv, o_kr = 0, _C_CKV - _C_CQ, _C_KR - _C_CQ

    cq = proj[:, o_cq:o_ckv]
    cqn = cq * lax.rsqrt(_mean_sq(cq, MLA_Q_RANK) + EPS) * gqa_ref[...]
    y = _dot(cqn.astype(BF16), wq_ref[...])
    lane = lax.broadcasted_iota(jnp.int32, (1, LANES), 1)
    rope_q = jnp.where(lane < MLA_ROPE_DIM, cos_t, sin_t) * gqr_ref[...]
    for hd in range(MLA_HEADS):
        yn = y[:, hd * _Q_HEAD_W: hd * _Q_HEAD_W + LANES]
        yr = y[:, hd * _Q_HEAD_W + LANES: (hd + 1) * _Q_HEAD_W]
        ssq = jnp.sum(yn * yn, axis=-1, keepdims=True) + 0.5 * jnp.sum(yr * yr, axis=-1, keepdims=True)
        rq = lax.rsqrt(ssq * (1.0 / MLA_QK_DIM) + EPS) * (MLA_QK_DIM ** -0.5 * LOG2E)
        qm_ref[0, :, hd * _Q_HEAD_W: hd * _Q_HEAD_W + LANES] = (yn * rq * gqn_ref[...]).astype(BF16)
        qm_ref[0, :, hd * _Q_HEAD_W + LANES: (hd + 1) * _Q_HEAD_W] = (yr * rq * rope_q).astype(BF16)

    kr0 = proj[:, o_kr:o_kr + LANES]
    z1 = kr0 + pltpu.roll(kr0, MLA_ROPE_DIM, axis=1)
    z2 = pltpu.roll(z1, MLA_ROPE_DIM // 2, axis=1)
    rk = lax.rsqrt(_mean_sq(z1, LANES) + EPS)
    k_rope = (rk * (z1 * (gk1_ref[...] * cos_t) + z2 * (gk2_ref[...] * sin_t))).astype(BF16)

    ckv = proj[:, o_ckv:o_kr]
    ckvn = ckv * lax.rsqrt(_mean_sq(ckv, MLA_KV_RANK) + EPS) * gkva_ref[...]
    kv = _dot(ckvn.astype(BF16), wkv_ref[...])
    for hd in range(MLA_HEADS):
        kn = kv[:, hd * _Q_HEAD_W: hd * _Q_HEAD_W + LANES]
        km_ref[0, :, hd * _Q_HEAD_W: hd * _Q_HEAD_W + LANES] = (
            kn * lax.rsqrt(_mean_sq(kn, MLA_NOPE_DIM) + EPS) * gkn_ref[...]).astype(BF16)
        km_ref[0, :, hd * _Q_HEAD_W + LANES: (hd + 1) * _Q_HEAD_W] = k_rope
        vm_ref[0, :, hd * LANES:(hd + 1) * LANES] = kv[:, hd * _Q_HEAD_W + LANES: (hd + 1) * _Q_HEAD_W].astype(BF16)

    proj_sb = _nt_dot(h, w1_ref[_C_QSB:_C_CQ, :])
    qkv_ref[0, :, 0:SB_WIDTH] = (proj_sb[:, _C_QSB:_C_KSB] * (SB_HEAD_DIM ** -0.5)).astype(BF16)
    qkv_ref[0, :, SB_WIDTH:3 * SB_WIDTH] = proj_sb[:, _C_KSB:_C_CQ].astype(BF16)

    for src, dst in ((wo_ref, wo_bf_ref), (wg_ref, wg_bf_ref), (wu_ref, wu_bf_ref), (wd_ref, wd_bf_ref)):
        dst[...] = src[...].astype(BF16)
    mod_late_ref[...] = _adaln_columns(c_ref[...], wada_ref[...], bada_ref[...])


def _inproj(x, pos4, mod3, gattn, w_in, gqa, wq, gkva, wkv, gqn, gqr, gkn, gk1, gk2, freq, sign, later_weights,
            c, w_ada, b_ada2):
    b, s, d = x.shape
    ts = PROJ_TS
    n_tok = s // ts
    n_early = mod3.shape[1] * d
    late_cols = (w_ada.shape[1] - n_early) // (b * n_tok)
    assert late_cols % LANES == 0 and late_cols * b * n_tok == w_ada.shape[1] - n_early
    late = lambda rows: pl.BlockSpec((rows, late_cols), lambda i, j: (0, n_early // late_cols + i * n_tok + j))
    const = lambda shape: pl.BlockSpec(shape, lambda i, j: (0,) * len(shape), pipeline_mode=pl.Buffered(1))
    tok = lambda w: pl.BlockSpec((1, ts, w), lambda i, j: (i, j, 0))

    def cast_slice(w):
        rows = w.shape[0]
        n_steps = b * n_tok
        while rows % (n_steps * BF16_SUBLANES):
            n_steps //= 2
        per = (b * n_tok) // n_steps
        return pl.BlockSpec((rows // n_steps, w.shape[1]), lambda i, j: ((i * n_tok + j) // per, 0))

    cast_specs = [cast_slice(w) for w in later_weights]
    return pl.pallas_call(
        _inproj_kernel,
        grid=(b, s // ts),
        in_specs=[
            tok(d),
            pl.BlockSpec((1, 1, ts // (LANES // (MLA_ROPE_DIM // 2)), LANES), lambda i, j: (i, j, 0, 0)),
            pl.BlockSpec((1, mod3.shape[1], d), lambda i, j: (i, 0, 0)),
            const((1, d)),
            const(w_in.shape),
            const((1, MLA_Q_RANK)),
            const(wq.shape),
            const((1, MLA_KV_RANK)),
            const(wkv.shape),
            const((1, LANES)), const((1, LANES)), const((1, LANES)), const((1, LANES)), const((1, LANES)),
            const((1, LANES)), const((1, LANES)),
        ] + cast_specs + [const(c.shape), late(d), late(1)],
        out_specs=[tok(3 * SB_WIDTH), tok(MLA_HEADS * _Q_HEAD_W), tok(MLA_HEADS * _Q_HEAD_W), tok(MLA_WIDTH)]
        + cast_specs + [pl.BlockSpec((b, late_cols), lambda i, j: (0, i * n_tok + j))],
        out_shape=[
            jax.ShapeDtypeStruct((b, s, 3 * SB_WIDTH), BF16),
            jax.ShapeDtypeStruct((b, s, MLA_HEADS * _Q_HEAD_W), BF16),
            jax.ShapeDtypeStruct((b, s, MLA_HEADS * _Q_HEAD_W), BF16),
            jax.ShapeDtypeStruct((b, s, MLA_WIDTH), BF16),
        ] + [jax.ShapeDtypeStruct(w.shape, BF16) for w in later_weights]
        + [jax.ShapeDtypeStruct((b, w_ada.shape[1] - n_early), F32)],
        scratch_shapes=[pltpu.VMEM((_N_PROJ, d), BF16)],
        compiler_params=pltpu.CompilerParams(
            dimension_semantics=("arbitrary", "arbitrary"), vmem_limit_bytes=VMEM_LIMIT_BYTES),
        name="inproj",
    )(x, pos4, mod3, gattn, w_in, gqa, wq, gkva, wkv, gqn, gqr, gkn, gk1, gk2, freq, sign, *later_weights,
      c, w_ada, b_ada2)


def _rows(j, t):
    return pl.ds(j * t, t)


def _sb_tasks(qb, q_ref, k_ref, v_ref, tri_ref, bias_ref, o_ref):
    t = ATT_T
    first = lax.broadcasted_iota(jnp.int32, (1, LANES), 1) < SB_HEAD_DIM
    q = q_ref[0, _rows(qb, t), :]
    qh = (jnp.where(first, q, jnp.zeros_like(q)), jnp.where(first, jnp.zeros_like(q), q))

    def softplus_bf16(zz):
        soft = jnp.log(1.0 + jnp.exp2(zz * LOG2E))
        return jnp.where(zz > SOFTPLUS_LINEAR_ABOVE, zz, soft).astype(BF16)

    half = t // 2

    def quadrants(full):
        return full[0:half, 0:half], full[half:t, :]

    def assemble(top_left, bottom):
        top = jnp.concatenate([top_left, jnp.zeros((half, t - half), top_left.dtype)], axis=1)
        return jnp.concatenate([top, bottom], axis=0)

    heads = (0, 1)
    q_both = jnp.concatenate(qh, axis=0)
    z, sp = {}, {}

    def score(j):
        zz_both = _nt_dot(q_both, k_ref[0, _rows(j, t), :])
        parts = []
        for hh in heads:
            zz = zz_both[hh * t:(hh + 1) * t]
            if j == qb:
                z[j, hh] = tuple(a + b for a, b in zip(quadrants(zz), quadrants(bias_ref[...])))
                parts.append(assemble(*(softplus_bf16(part) for part in z[j, hh])))
            else:
                z[j, hh] = zz
                parts.append(softplus_bf16(zz))
        sp[j] = jnp.concatenate(parts, axis=0)

    run, acc = {}, {}

    def weigh(j):
        cin_both = _dot(sp[j], tri_ref[...])
        ws = []
        for hh in heads:
            cin = cin_both[hh * t:(hh + 1) * t]
            if j == qb:
                ws.append(assemble(*(jnp.exp2((zp - cp) * LOG2E).astype(BF16)
                                     for zp, cp in zip(z[j, hh], quadrants(cin)))))
            else:
                arg = z[j, hh] - cin - jnp.concatenate([run[hh]] * (t // LANES), axis=1)
                ws.append(jnp.exp2(arg * LOG2E).astype(BF16))
            tot = jnp.broadcast_to(cin[:, 0:1], (t, LANES))
            run[hh] = tot if j == qb else run[hh] + tot
        pv_both = _dot(jnp.concatenate(ws, axis=0), v_ref[0, _rows(j, t), :])
        for hh in heads:
            pv = pv_both[hh * t:(hh + 1) * t]
            acc[hh] = pv if j == qb else acc[hh] + pv
        if j == 0:
            o_ref[0, _rows(qb, t), :] = jnp.where(first, acc[0], acc[1]).astype(BF16)

    score_tasks = [lambda j=j: score(j) for j in range(qb + 1)]
    weigh_tasks = [lambda j=j: weigh(j) for j in range(qb, -1, -1)]
    return score_tasks, weigh_tasks


def _mla_tasks(qb, q_ref, k_ref, vaug_ref, bias_ref, o_ref):
    t = ATT_T
    n_col = t // LANES
    st = {"s": {}, "colmax": None, "acc": None}

    def score(j):
        sj = _nt_dot(q_ref[0, _rows(qb, t), :], k_ref[0, _rows(j, t), :])
        if j == qb:
            sj = sj + bias_ref[...]
        st["s"][j] = sj
        for cb in range(n_col):
            blk = sj[:, cb * LANES:(cb + 1) * LANES]
            st["colmax"] = blk if st["colmax"] is None else jnp.maximum(st["colmax"], blk)

    def accumulate(j):
        if j == 0:
            m = jnp.broadcast_to(jnp.max(st["colmax"], axis=-1, keepdims=True), (t, LANES))
            st["m"] = jnp.concatenate([m] * n_col, axis=1)
        p = jnp.exp2(st["s"][j] - st["m"]).astype(BF16)
        pv = _dot(p, vaug_ref[_rows(j, t), :])
        st["acc"] = pv if j == 0 else st["acc"] + pv
        if j == qb:
            acc = st["acc"]
            o_ref[0, _rows(qb, t), :] = (acc[:, 0:LANES] / acc[:, LANES:2 * LANES]).astype(BF16)

    return ([lambda j=j: score(j) for j in range(qb + 1)]
            + [lambda j=j: accumulate(j) for j in range(qb + 1)])


def _merge_tasks(lists):
    keyed = [((i + 0.5) / len(tasks), n, i, task) for n, tasks in enumerate(lists) for i, task in enumerate(tasks)]
    return [task for _, _, _, task in sorted(keyed, key=lambda e: e[:3])]


def _attn_kernel(q_ref, k_ref, v_ref, tri_ref, bias_s_ref, qm_ref, km_ref, vm_ref, bias_c_ref,
                 osb_ref, omla_ref, vaug_ref):
    nq = k_ref.shape[1] // ATT_T
    vaug_ref[:, 0:LANES] = vm_ref[0]
    vaug_ref[:, LANES:2 * LANES] = jnp.ones((vm_ref.shape[1], LANES), BF16)
    order = [qb for c in range(nq // 2) for qb in (c, nq - 1 - c)]
    pending = [[], []]
    for qb in order + [None, None]:
        lists = list(pending[0])
        pending = [pending[1], []]
        if qb is not None:
            score_tasks, weigh_tasks = _sb_tasks(qb, q_ref, k_ref, v_ref, tri_ref, bias_s_ref, osb_ref)
            lists.append(score_tasks)
            lists.append(_mla_tasks(qb, qm_ref, km_ref, vaug_ref, bias_c_ref, omla_ref))
            pending[1] = [weigh_tasks]
        for task in _merge_tasks([tasks for tasks in lists if tasks]):
            task()


def _attention(qkv, qm, km, vm, tri, bias_strict, bias_causal):
    b, s, _ = qkv.shape
    n_grp = SB_WIDTH // LANES
    const = lambda shape: pl.BlockSpec(shape, lambda bi, g: (0,) * len(shape), pipeline_mode=pl.Buffered(1))
    lane_blk = lambda w, off: pl.BlockSpec((1, s, w), lambda bi, g: (bi, 0, off + g))
    return pl.pallas_call(
        _attn_kernel,
        grid=(b, n_grp),
        in_specs=[
            lane_blk(LANES, 0), lane_blk(LANES, n_grp), lane_blk(LANES, 2 * n_grp),
            const(tri.shape), const(bias_strict.shape),
            lane_blk(_Q_HEAD_W, 0), lane_blk(_Q_HEAD_W, 0), lane_blk(LANES, 0),
            const(bias_causal.shape),
        ],
        out_specs=[lane_blk(LANES, 0), lane_blk(LANES, 0)],
        out_shape=[jax.ShapeDtypeStruct((b, s, SB_WIDTH), BF16), jax.ShapeDtypeStruct((b, s, MLA_WIDTH), BF16)],
        scratch_shapes=[pltpu.VMEM((s, 2 * LANES), BF16)],
        compiler_params=pltpu.CompilerParams(
            dimension_semantics=("arbitrary", "arbitrary"), vmem_limit_bytes=VMEM_LIMIT_BYTES),
        name="attn",
    )(qkv, qkv, qkv, tri, bias_strict, qm, km, vm, bias_causal)


def _outffn_kernel(x_ref, osb_ref, omla_ref, mod_ref, gsb_ref, gmla_ref, gffn_ref,
                   wo_ref, wg_ref, wu_ref, wd_ref, out_ref, act_ref):
    gate1 = mod_ref[0, 0:1, :]
    shift2 = mod_ref[0, 1:2, :]
    scale2 = mod_ref[0, 2:3, :]
    gate2 = mod_ref[0, 3:4, :]

    osb = osb_ref[0].astype(F32)
    omla = omla_ref[0].astype(F32)
    nsb = osb * lax.rsqrt(_mean_sq(osb, SB_WIDTH) + EPS) * gsb_ref[...]
    nmla = omla * lax.rsqrt(_mean_sq(omla, MLA_WIDTH) + EPS) * gmla_ref[...]
    mixed = jnp.concatenate([nsb.astype(BF16), nmla.astype(BF16)], axis=1)
    x1 = x_ref[0] + gate1 * _dot(mixed, wo_ref[...])

    h2 = (x1 * lax.rsqrt(_mean_sq(x1, D_MODEL) + EPS)) * (gffn_ref[...] * (1.0 + scale2)) + shift2
    h2 = h2.astype(BF16)
    d_ff = wg_ref.shape[1]
    for cidx in range(d_ff // FFN_TN):
        sl = slice(cidx * FFN_TN, (cidx + 1) * FFN_TN)
        g = _dot(h2, wg_ref[:, sl])
        u = _dot(h2, wu_ref[:, sl])
        act_ref[:, sl] = (g / (1.0 + jnp.exp(-g)) * u).astype(BF16)
    out_ref[0] = x1 + gate2 * _dot(act_ref[...], wd_ref[...])


def _outffn(x, osb, omla, mod3, gsb, gmla, gffn, wo, wg, wu, wd):
    b, s, d = x.shape
    tm = FFN_TM
    d_ff = wg.shape[1]
    const = lambda shape: pl.BlockSpec(shape, lambda i, j: (0,) * len(shape), pipeline_mode=pl.Buffered(1))
    tok = lambda w: pl.BlockSpec((1, tm, w), lambda i, j: (i, j, 0))
    return pl.pallas_call(
        _outffn_kernel,
        grid=(b, s // tm),
        in_specs=[
            tok(d), tok(SB_WIDTH), tok(MLA_WIDTH),
            pl.BlockSpec((1, mod3.shape[1], d), lambda i, j: (i, 0, 0)),
            const((1, SB_WIDTH)), const((1, MLA_WIDTH)), const((1, d)),
            const(wo.shape), const(wg.shape), const(wu.shape), const(wd.shape),
        ],
        out_specs=tok(d),
        out_shape=jax.ShapeDtypeStruct((b, s, d), F32),
        scratch_shapes=[pltpu.VMEM((tm, d_ff), BF16)],
        compiler_params=pltpu.CompilerParams(
            dimension_semantics=("arbitrary", "arbitrary"), vmem_limit_bytes=VMEM_LIMIT_BYTES),
        name="outffn",
    )(x, osb, omla, mod3, gsb, gmla, gffn, wo, wg, wu, wd)


def _swap_halves(a):
    half = a.shape[-1] // 2
    return jnp.concatenate([a[..., half:], a[..., :half]], axis=-1)


def _layer_params(l, w_q_up, w_kv_up, q_norm, k_rope_norm):
    d_q = w_q_up.shape[1]
    wq4 = w_q_up[l].reshape(d_q, MLA_HEADS, MLA_QK_DIM)
    wq = jnp.concatenate([wq4, _swap_halves(wq4[:, :, MLA_NOPE_DIM:])], axis=2)
    wq = wq.reshape(d_q, MLA_HEADS * _Q_HEAD_W).astype(BF16)

    wkv = w_kv_up[l].astype(BF16)

    gqn = q_norm[l][:MLA_NOPE_DIM].reshape(1, LANES)
    gr = q_norm[l][MLA_NOPE_DIM:]
    gqr = jnp.concatenate([gr, _swap_halves(gr)]).reshape(1, LANES)
    gk = k_rope_norm[l]
    gk1 = jnp.concatenate([gk, gk]).reshape(1, LANES)
    gk2 = jnp.concatenate([_swap_halves(gk), _swap_halves(gk)]).reshape(1, LANES)
    return wq, wkv, gqn, gqr, gk1, gk2


def _constants():
    half = MLA_ROPE_DIM // 2
    freqs = 1.0 / (ROPE_THETA ** (jnp.arange(half, dtype=F32) / half))
    freq = jnp.tile(freqs, LANES // half).reshape(1, LANES)
    sign = np.tile(np.concatenate([-np.ones(half), np.ones(half)]), LANES // (2 * half))
    sign = jnp.asarray(sign, F32).reshape(1, LANES)
    t = ATT_T
    row = np.arange(t)[:, None]
    col = np.arange(t)[None, :]
    tri = jnp.asarray((row >= col).astype(np.float32), BF16)
    bias_strict = jnp.asarray(np.where(col < row, 0.0, MASK_BIAS), F32)
    bias_causal = jnp.asarray(np.where(col <= row, 0.0, MASK_BIAS), F32)
    return freq, sign, tri, bias_strict, bias_causal


def kernel(x, c, positions, w_ada, b_ada, norm_attn, norm_ffn, w_in, q_a_norm, w_q_up, kv_a_norm, w_kv_up,
           q_norm, k_nope_norm, k_rope_norm, out_norm_sb, out_norm_mla, w_out, w_gate, w_up, w_down):
    b, s, d = x.shape
    depth = w_ada.shape[0]
    freq, sign, tri, bias_strict, bias_causal = _constants()
    n_grp = LANES // (MLA_ROPE_DIM // 2)
    rows = PROJ_TS // n_grp
    pos4 = positions.astype(F32).reshape(b, s // PROJ_TS, n_grp, rows).transpose(0, 1, 3, 2)
    pos4 = jnp.repeat(pos4, LANES // n_grp, axis=-1)
    for l in range(depth):
        b_ada2 = b_ada[l].reshape(1, -1)
        mod_early = _adaln(c, w_ada[l], b_ada2, N_MOD_EARLY * d).reshape(b, N_MOD_EARLY, d)
        wq, wkv, gqn, gqr, gk1, gk2 = _layer_params(l, w_q_up, w_kv_up, q_norm, k_rope_norm)
        qkv, qm, km, vm, wo, wg, wu, wd, mod_late = _inproj(
            x, pos4, mod_early, norm_attn[l].reshape(1, d), jnp.swapaxes(w_in[l], 0, 1), q_a_norm[l].reshape(1, -1), wq,
            kv_a_norm[l].reshape(1, -1), wkv, gqn, gqr, k_nope_norm[l].reshape(1, LANES), gk1, gk2, freq, sign,
            (w_out[l], w_gate[l], w_up[l], w_down[l]), c, w_ada[l], b_ada2)
        osb, omla = _attention(qkv, qm, km, vm, tri, bias_strict, bias_causal)
        x = _outffn(x, osb, omla, mod_late.reshape(b, N_MOD - N_MOD_EARLY, d), out_norm_sb[l].reshape(1, -1),
                    out_norm_mla[l].reshape(1, -1), norm_ffn[l].reshape(1, d), wo, wg, wu, wd)
    return x
```
